```python
import math
import jax, jax.numpy as jnp
from jax import lax
import numpy as np

D_MODEL = 2048
BATCH = 2
SEQ = 4096
DEPTH = 1
DEC_BATCH = 128
DEC_SEQ = 1
PAST_LEN = 2048
PAGE_SIZE = 128

MIX_WIDTH = D_MODEL
A_WIDTH = MIX_WIDTH // 2
B_WIDTH = MIX_WIDTH - A_WIDTH
CHUNK = 128
A_GROUP = 128
A_GROUPS = A_WIDTH // A_GROUP
HEAD_DIM = 128
SB_HEADS = B_WIDTH // HEAD_DIM
Q_BLOCK = 128
SB_BIAS_INIT = -6.0
D_FF = 256 * ((8 * D_MODEL // 3 + 255) // 256)
IN_COLS = 2 * A_WIDTH + 3 * B_WIDTH
ALPHA = (2 * DEPTH) ** 0.25
BETA = (8 * DEPTH) ** -0.25
LN_EPS = 1e-5

kernel_name = "hybrid_gmlp_stickbreaking_macaron_deepnorm_step"


def layer_norm(x, g, b):
    xf = x.astype(jnp.float32)
    mu = jnp.mean(xf, axis=-1, keepdims=True)
    var = jnp.mean(jnp.square(xf - mu), axis=-1, keepdims=True)
    y = (xf - mu) * lax.rsqrt(var + LN_EPS)
    return (y * g.astype(jnp.float32) + b.astype(jnp.float32)).astype(x.dtype)


def swiglu(x, w_gate, w_up, w_down):
    return (jax.nn.silu(x @ w_gate) * (x @ w_up)) @ w_down


def macaron_half(x, w_gate, w_up, w_down, g, b):
    return layer_norm(ALPHA * x + 0.5 * swiglu(x, w_gate, w_up, w_down), g, b)


def mixer_inputs(h, w_in, g_v, b_v):
    z = h @ w_in
    uv = jax.nn.gelu(z[..., :2 * A_WIDTH])
    u = uv[..., :A_WIDTH]
    v = layer_norm(uv[..., A_WIDTH:], g_v, b_v)
    q, k, vb = jnp.split(z[..., 2 * A_WIDTH:], 3, axis=-1)
    heads = lambda t: t.reshape(t.shape[:-1] + (SB_HEADS, HEAD_DIM))
    return u, v, heads(q), heads(k), heads(vb)


def chunk_spatial_gate(u, v, w_s, b_s):
    n, t, _ = v.shape
    n_chunks = -(-t // CHUNK)
    pad = n_chunks * CHUNK - t
    vp = jnp.pad(v, ((0, 0), (0, pad), (0, 0))).reshape(n, n_chunks, CHUNK, A_GROUPS, A_GROUP)
    w = jnp.tril(w_s)
    mixed = jnp.einsum('gts,ncsgd->nctgd', w, vp) + b_s.T[None, None, :, :, None]
    mixed = mixed.reshape(n, n_chunks * CHUNK, A_WIDTH)[:, :t]
    return u * mixed


def stick_breaking(q, k, v, sb_bias, q_pos, k_pos):
    z = jnp.einsum('nqhd,nkhd->nhqk', q, k).astype(jnp.float32) * (HEAD_DIM ** -0.5)
    z = z + sb_bias.astype(jnp.float32)[None, :, None, None]
    mask = k_pos[None, :] < q_pos[:, None]
    log_beta = jax.nn.log_sigmoid(z)
    log_1m = jnp.where(mask, jax.nn.log_sigmoid(-z), 0.0)
    after = lax.cumsum(log_1m, axis=3, reverse=True) - log_1m
    a = jnp.where(mask, jnp.exp(log_beta + after), 0.0)
    return jnp.einsum('nhqk,nkhd->nqhd', a.astype(v.dtype), v)


def stick_breaking_prompt(q, k, v, sb_bias):
    n, t = q.shape[0], q.shape[1]
    n_blocks = t // Q_BLOCK
    k_pos = jnp.arange(t)

    def block(i):
        start = i * Q_BLOCK
        qb = lax.dynamic_slice_in_dim(q, start, Q_BLOCK, axis=1)
        return stick_breaking(qb, k, v, sb_bias, start + jnp.arange(Q_BLOCK), k_pos)

    out = lax.map(block, jnp.arange(n_blocks))
    return out.transpose(1, 0, 2, 3, 4).reshape(n, t, SB_HEADS, HEAD_DIM)


def gather_pages(cache, page_table):
    g = jnp.take(cache, page_table, axis=0)
    return g.reshape(g.shape[0], g.shape[1] * g.shape[2], g.shape[3], g.shape[4])


def post_mix(h1, a_out, b_out, w_o, ln2_g, ln2_b, w_gate, w_up, w_down, ln3_g, ln3_b):
    mix = jnp.concatenate([a_out, b_out.reshape(b_out.shape[:2] + (B_WIDTH,))], axis=-1) @ w_o
    h2 = layer_norm(ALPHA * h1 + mix, ln2_g, ln2_b)
    return macaron_half(h2, w_gate, w_up, w_down, ln3_g, ln3_b)


def setup_inputs(seed: int = 0) -> dict:
    key = jax.random.key(seed)
    ks = jax.random.split(key, 24)
    n_pages = PAST_LEN // PAGE_SIZE
    n_phys = (DEC_BATCH * n_pages * 5) // 4
    nrm = lambda k, shape, s: jax.random.normal(k, shape, jnp.float32) * s
    gain = lambda k, shape: 1.0 + nrm(k, shape, 0.02)
    w_in = nrm(ks[5], (DEPTH, D_MODEL, IN_COLS), D_MODEL ** -0.5)
    w_in = w_in.at[..., 2 * A_WIDTH + 2 * B_WIDTH:].multiply(BETA)
    page_table = jax.random.permutation(ks[4], n_phys)[:DEC_BATCH * n_pages]
    page_table = page_table.reshape(DEC_BATCH, n_pages).astype(jnp.int32)
    return {
        "x_prompt": nrm(ks[0], (BATCH, SEQ, D_MODEL), 1.0),
        "x_sample": nrm(ks[1], (DEC_BATCH, DEC_SEQ, D_MODEL), 1.0),
        "cache_k": nrm(ks[2], (DEPTH, n_phys, PAGE_SIZE, SB_HEADS, HEAD_DIM), 1.0),
        "cache_v": nrm(ks[3], (DEPTH, n_phys, PAGE_SIZE, SB_HEADS, HEAD_DIM), BETA),
        "page_table": page_table,
        "ln1_g": gain(ks[6], (DEPTH, D_MODEL)),
        "ln1_b": nrm(ks[7], (DEPTH, D_MODEL), 0.02),
        "ffn1_w_gate": nrm(ks[8], (DEPTH, D_MODEL, D_FF), D_MODEL ** -0.5),
        "ffn1_w_up": nrm(ks[9], (DEPTH, D_MODEL, D_FF), D_MODEL ** -0.5 * BETA),
        "ffn1_w_down": nrm(ks[10], (DEPTH, D_FF, D_MODEL), D_FF ** -0.5 * BETA),
        "w_in": w_in,
        "gmlp_ln_g": gain(ks[11], (DEPTH, A_WIDTH)),
        "gmlp_ln_b": nrm(ks[12], (DEPTH, A_WIDTH), 0.02),
        "gmlp_w_s": nrm(ks[13], (DEPTH, A_GROUPS, CHUNK, CHUNK), CHUNK ** -0.5),
        "gmlp_b_s": gain(ks[14], (DEPTH, A_GROUPS, CHUNK)),
        "sb_bias": SB_BIAS_INIT + nrm(ks[23], (DEPTH, SB_HEADS), 0.1),
        "w_o": nrm(ks[15], (DEPTH, MIX_WIDTH, D_MODEL), MIX_WIDTH ** -0.5 * BETA),
        "ln2_g": gain(ks[16], (DEPTH, D_MODEL)),
        "ln2_b": nrm(ks[17], (DEPTH, D_MODEL), 0.02),
        "ffn2_w_gate": nrm(ks[18], (DEPTH, D_MODEL, D_FF), D_MODEL ** -0.5),
        "ffn2_w_up": nrm(ks[19], (DEPTH, D_MODEL, D_FF), D_MODEL ** -0.5 * BETA),
        "ffn2_w_down": nrm(ks[20], (DEPTH, D_FF, D_MODEL), D_FF ** -0.5 * BETA),
        "ln3_g": gain(ks[21], (DEPTH, D_MODEL)),
        "ln3_b": nrm(ks[22], (DEPTH, D_MODEL), 0.02),
    }


def reference(x_prompt, x_sample, cache_k, cache_v, page_table,
              ln1_g, ln1_b, ffn1_w_gate, ffn1_w_up, ffn1_w_down,
              w_in, gmlp_ln_g, gmlp_ln_b, gmlp_w_s, gmlp_b_s, sb_bias, w_o,
              ln2_g, ln2_b, ffn2_w_gate, ffn2_w_up, ffn2_w_down, ln3_g, ln3_b):
    xp, xs = x_prompt, x_sample
    past = page_table.shape[1] * cache_k.shape[2]
    n_new = xs.shape[1]
    kp_l, vp_l, ks_l, vs_l, cv_l = [], [], [], [], []
    for l in range(DEPTH):
        hp = macaron_half(xp, ffn1_w_gate[l], ffn1_w_up[l], ffn1_w_down[l], ln1_g[l], ln1_b[l])
        u, v, q, k, vb = mixer_inputs(hp, w_in[l], gmlp_ln_g[l], gmlp_ln_b[l])
        a_out = chunk_spatial_gate(u, v, gmlp_w_s[l], gmlp_b_s[l])
        b_out = stick_breaking_prompt(q, k, vb, sb_bias[l])
        xp = post_mix(hp, a_out, b_out, w_o[l], ln2_g[l], ln2_b[l],
                      ffn2_w_gate[l], ffn2_w_up[l], ffn2_w_down[l], ln3_g[l], ln3_b[l])
        kp_l.append(k)
        vp_l.append(vb)
        hs = macaron_half(xs, ffn1_w_gate[l], ffn1_w_up[l], ffn1_w_down[l], ln1_g[l], ln1_b[l])
        u_s, v_s, q_s, k_s, vb_s = mixer_inputs(hs, w_in[l], gmlp_ln_g[l], gmlp_ln_b[l])
        a_s = chunk_spatial_gate(u_s, v_s, gmlp_w_s[l], gmlp_b_s[l])
        k_all = jnp.concatenate([gather_pages(cache_k[l], page_table), k_s], axis=1)
        v_all = jnp.concatenate([gather_pages(cache_v[l], page_table), vb_s], axis=1)
        b_s = stick_breaking(q_s, k_all, v_all, sb_bias[l],
                             past + jnp.arange(n_new), jnp.arange(past + n_new))
        xs = post_mix(hs, a_s, b_s, w_o[l], ln2_g[l], ln2_b[l],
                      ffn2_w_gate[l], ffn2_w_up[l], ffn2_w_down[l], ln3_g[l], ln3_b[l])
        ks_l.append(k_s)
        vs_l.append(vb_s)
        cv_l.append(v_s)
    k_prompt = jnp.stack(kp_l)
    v_prompt = jnp.stack(vp_l)
    k_sample = jnp.stack(ks_l)
    v_sample = jnp.stack(vs_l)
    chunk_v_sample = jnp.stack(cv_l)
    return (xp, xs, k_prompt, v_prompt, k_sample, v_sample, chunk_v_sample)
```

```python
import functools

import jax
import jax.numpy as jnp
from jax import lax
from jax.experimental import pallas as pl
from jax.experimental.pallas import tpu as pltpu

LN_EPS = 1e-5
HEAD_DIM = 128
A_GROUP = 128
CHUNK = 128

V7X_VMEM_LIMIT_BYTES = 56 * 1024 * 1024

F32 = jnp.float32
BF16 = jnp.bfloat16


def _params(*semantics):
    return pltpu.CompilerParams(dimension_semantics=semantics,
                                vmem_limit_bytes=V7X_VMEM_LIMIT_BYTES)


def _layer_norm(y, g, b):
    mu = jnp.mean(y, axis=-1, keepdims=True)
    d = y - mu
    var = jnp.mean(d * d, axis=-1, keepdims=True)
    return d * lax.rsqrt(var + LN_EPS) * g + b


def _dot(a, b):
    return jnp.dot(a, b, preferred_element_type=F32)


def _dot_nt(a, b):
    return lax.dot_general(a, b, (((1,), (1,)), ((), ())), preferred_element_type=F32)


def _stick_logs(z):
    l1m = -(jnp.maximum(z, 0.0) + jnp.log1p(jnp.exp(-jnp.abs(z))))
    return z + l1m, l1m


def _split_bf16(x):
    hi = x.astype(BF16)
    lo = (x - hi.astype(F32)).astype(BF16)
    return hi, lo


def _strict_after_matrix(n):
    j = lax.broadcasted_iota(jnp.int32, (n, n), 0)
    s = lax.broadcasted_iota(jnp.int32, (n, n), 1)
    return (j > s).astype(BF16)


def _ffn_ln_kernel(x_ref, wg_ref, wu_ref, wd_ref, g_ref, b_ref, o_ref, xb_ref, *, alpha):
    j = pl.program_id(1)

    @pl.when(j == 0)
    def _():
        xb_ref[...] = x_ref[...].astype(BF16)
        o_ref[...] = jnp.zeros_like(o_ref)

    xb = xb_ref[...]
    gate = _dot(xb, wg_ref[...])
    up = _dot(xb, wu_ref[...])
    h = (jax.nn.silu(gate) * up).astype(BF16)
    o_ref[...] += _dot(h, wd_ref[...])

    @pl.when(j == pl.num_programs(1) - 1)
    def _():
        y = alpha * x_ref[...] + 0.5 * o_ref[...]
        o_ref[...] = _layer_norm(y, g_ref[...], b_ref[...])


def _ffn_ln(x, wg, wu, wd, g, b, *, alpha, tm, tf):
    m, d = x.shape
    d_ff = wg.shape[1]
    return pl.pallas_call(
        functools.partial(_ffn_ln_kernel, alpha=alpha),
        grid=(m // tm, d_ff // tf),
        in_specs=[
            pl.BlockSpec((tm, d), lambda i, j: (i, 0)),
            pl.BlockSpec((d, tf), lambda i, j: (0, j)),
            pl.BlockSpec((d, tf), lambda i, j: (0, j)),
            pl.BlockSpec((tf, d), lambda i, j: (j, 0)),
            pl.BlockSpec((1, d), lambda i, j: (0, 0)),
            pl.BlockSpec((1, d), lambda i, j: (0, 0)),
        ],
        out_specs=pl.BlockSpec((tm, d), lambda i, j: (i, 0)),
        out_shape=jax.ShapeDtypeStruct((m, d), F32),
        scratch_shapes=[pltpu.VMEM((tm, d), BF16)],
        compiler_params=_params("parallel", "arbitrary"),
        name="ffn_ln",
    )(x, wg, wu, wd, g, b)


def _in_proj_kernel(h_ref, w_ref, lg_ref, lb_ref,
                    u_ref, v_ref, q_ref, k32_ref, vb32_ref, k16_ref, vb16_ref,
                    hb_ref, *, q_scale):
    c = pl.program_id(1)

    @pl.when(c == 0)
    def _():
        hb_ref[...] = h_ref[...].astype(BF16)

    z = _dot(hb_ref[...], w_ref[...])

    @pl.when(c == 0)
    def _():
        u_ref[...] = jax.nn.gelu(z)

    @pl.when(c == 1)
    def _():
        v_ref[...] = _layer_norm(jax.nn.gelu(z), lg_ref[...], lb_ref[...]).astype(v_ref.dtype)

    @pl.when(c == 2)
    def _():
        q_ref[...] = (z * q_scale).astype(BF16)

    @pl.when(c == 3)
    def _():
        k32_ref[...] = z
        k16_ref[...] = z.astype(BF16)

    @pl.when(c == 4)
    def _():
        vb32_ref[...] = z
        vb16_ref[...] = z.astype(BF16)


def _in_proj(h, w_in, lg, lb, *, tm, v_dtype):
    m, d = h.shape
    w = w_in.shape[1] // 5
    row = lambda i, c: (i, 0)
    out = lambda dt: jax.ShapeDtypeStruct((m, w), dt)
    return pl.pallas_call(
        functools.partial(_in_proj_kernel, q_scale=HEAD_DIM ** -0.5),
        grid=(m // tm, 5),
        in_specs=[
            pl.BlockSpec((tm, d), row),
            pl.BlockSpec((d, w), lambda i, c: (0, c)),
            pl.BlockSpec((1, w), lambda i, c: (0, 0)),
            pl.BlockSpec((1, w), lambda i, c: (0, 0)),
        ],
        out_specs=[pl.BlockSpec((tm, w), row)] * 7,
        out_shape=[out(F32), out(v_dtype), out(BF16), out(F32), out(F32), out(BF16), out(BF16)],
        scratch_shapes=[pltpu.VMEM((tm, d), BF16)],
        compiler_params=_params("parallel", "arbitrary"),
        name="in_proj",
    )(h, w_in, lg, lb)


def _gate_kernel(u_ref, v_ref, ws_ref, bst_ref, a_ref, *, chunks):
    n_groups = ws_ref.shape[0]
    t = lax.broadcasted_iota(jnp.int32, (CHUNK, CHUNK), 0)
    s = lax.broadcasted_iota(jnp.int32, (CHUNK, CHUNK), 1)
    causal = s <= t
    for g in range(n_groups):
        w = jnp.where(causal, ws_ref[g], 0.0).astype(BF16)
        bias = bst_ref[:, g:g + 1]
        cols = slice(g * A_GROUP, (g + 1) * A_GROUP)
        for c in range(chunks):
            rows = slice(c * CHUNK, (c + 1) * CHUNK)
            mixed = _dot(w, v_ref[rows, cols]) + bias
            a_ref[rows, cols] = (u_ref[rows, cols] * mixed).astype(a_ref.dtype)


def _gate(u, v, w_s, b_s_t, *, chunks):
    m, w = u.shape
    rows = chunks * CHUNK
    return pl.pallas_call(
        functools.partial(_gate_kernel, chunks=chunks),
        grid=(m // rows,),
        in_specs=[
            pl.BlockSpec((rows, w), lambda i: (i, 0)),
            pl.BlockSpec((rows, w), lambda i: (i, 0)),
            pl.BlockSpec(w_s.shape, lambda i: (0, 0, 0)),
            pl.BlockSpec(b_s_t.shape, lambda i: (0, 0)),
        ],
        out_specs=pl.BlockSpec((rows, w), lambda i: (i, 0)),
        out_shape=jax.ShapeDtypeStruct((m, w), BF16),
        compiler_params=_params("parallel"),
        name="gate",
    )(u, v, w_s, b_s_t)


def _sample_gate_kernel(u_ref, v_ref, w0_ref, b0_ref, a_ref):
    a_ref[...] = (u_ref[...] * (w0_ref[...] * v_ref[...] + b0_ref[...])).astype(a_ref.dtype)


def _sample_gate(u, v, w0, b0):
    m, w = u.shape
    full = lambda shape: pl.BlockSpec(shape, lambda i: (0, 0))
    return pl.pallas_call(
        _sample_gate_kernel,
        grid=(1,),
        in_specs=[full((m, w)), full((m, w)), full((1, w)), full((1, w))],
        out_specs=full((m, w)),
        out_shape=jax.ShapeDtypeStruct((m, w), BF16),
        compiler_params=_params("arbitrary"),
        name="sample_gate",
    )(u, v, w0, b0)


def _sb_prompt_kernel(bias_ref, q_ref, k_ref, v_ref, o_ref, *, heads, tq):
    hg = pl.program_id(1)
    qi = pl.program_id(2)
    after = _strict_after_matrix(tq)
    t = lax.broadcasted_iota(jnp.int32, (tq, tq), 0)
    s = lax.broadcasted_iota(jnp.int32, (tq, tq), 1)
    diag_mask = s < t

    def tile(kb, carry, masked):
        start = pl.multiple_of(kb * tq, tq)
        new = []
        for hh in range(heads):
            run, acc = carry[hh]
            cols = slice(hh * HEAD_DIM, (hh + 1) * HEAD_DIM)
            z = _dot_nt(q_ref[:, cols], k_ref[pl.ds(start, tq), cols]) + bias_ref[hg * heads + hh]
            lb, l1m = _stick_logs(z)
            if masked:
                l1m = jnp.where(diag_mask, l1m, 0.0)
            hi, lo = _split_bf16(l1m)
            aft = _dot(hi, after) + _dot(lo, after)
            a = jnp.exp(lb + aft + run)
            if masked:
                a = jnp.where(diag_mask, a, 0.0)
            acc = acc + _dot(a.astype(BF16), v_ref[pl.ds(start, tq), cols])
            run = run + jnp.sum(l1m, axis=1, keepdims=True)
            new.append((run, acc))
        return tuple(new)

    init = tuple((jnp.zeros((tq, 1), F32), jnp.zeros((tq, HEAD_DIM), F32)) for _ in range(heads))
    carry = tile(qi, init, True)
    carry = lax.fori_loop(0, qi, lambda i, c: tile(qi - 1 - i, c, False), carry)
    for hh in range(heads):
        o_ref[:, hh * HEAD_DIM:(hh + 1) * HEAD_DIM] = carry[hh][1].astype(o_ref.dtype)


def _sb_prompt(q, k, v, sb_bias, *, n, t, heads, tq):
    m, w = q.shape
    gw = heads * HEAD_DIM
    nq = t // tq
    return pl.pallas_call(
        functools.partial(_sb_prompt_kernel, heads=heads, tq=tq),
        grid=(n, w // gw, nq),
        in_specs=[
            pl.BlockSpec(memory_space=pltpu.SMEM),
            pl.BlockSpec((tq, gw), lambda b, h, i: (b * nq + i, h)),
            pl.BlockSpec((t, gw), lambda b, h, i: (b, h)),
            pl.BlockSpec((t, gw), lambda b, h, i: (b, h)),
        ],
        out_specs=pl.BlockSpec((tq, gw), lambda b, h, i: (b * nq + i, h)),
        out_shape=jax.ShapeDtypeStruct((m, w), BF16),
        compiler_params=_params("parallel", "parallel", "arbitrary"),
        name="sb_prompt",
    )(sb_bias, q, k, v)


def _sb_sample_kernel(pt_ref, q_ref, kn_ref, vn_ref, bias_ref, *refs, pages, past):
    del pt_ref
    k_refs = refs[:pages]
    v_refs = refs[pages:2 * pages]
    o_ref, run_ref, acc_ref = refs[2 * pages:]
    g = pl.program_id(1)
    n_heads, hd = acc_ref.shape
    page_size = k_refs[0].shape[1]
    q_pos = past
    bias = bias_ref[...]
    q = q_ref[0]

    @pl.when(g == 0)
    def _():
        z = jnp.sum(q.astype(F32) * kn_ref[0], axis=1, keepdims=True) + bias
        lb, l1m = _stick_logs(z)
        seen = jnp.full(z.shape, past, jnp.int32) < q_pos
        run_ref[...] = jnp.where(seen, l1m, 0.0)
        acc_ref[...] = jnp.where(seen, jnp.exp(lb), 0.0) * vn_ref[0]

    after = _strict_after_matrix(page_size)
    head = lax.broadcasted_iota(jnp.int32, (n_heads, page_size), 0)
    lane = lax.broadcasted_iota(jnp.int32, (n_heads, page_size), 1)
    first_page = (pl.num_programs(1) - 1 - g) * pages
    run = run_ref[...]
    acc = acc_ref[...]
    for j in reversed(range(pages)):
        z = jnp.zeros((n_heads, page_size), F32)
        for h in range(n_heads):
            z = jnp.where(head == h, _dot_nt(q, k_refs[j][0, :, h, :].astype(BF16)), z)
        z = z + bias
        lb, l1m = _stick_logs(z)
        seen = (first_page + j) * page_size + lane < q_pos
        l1m = jnp.where(seen, l1m, 0.0)
        hi, lo = _split_bf16(l1m)
        aft = _dot(hi, after) + _dot(lo, after)
        a = jnp.where(seen, jnp.exp(lb + aft + run), 0.0).astype(BF16)
        for h in range(n_heads):
            acc = acc + jnp.where(head == h, _dot(a, v_refs[j][0, :, h, :].astype(BF16)), 0.0)
        run = run + jnp.sum(l1m, axis=1, keepdims=True)
    run_ref[...] = run
    acc_ref[...] = acc

    @pl.when(g == pl.num_programs(1) - 1)
    def _():
        o_ref[0] = acc.astype(o_ref.dtype)


def _sb_sample(page_table, q, k_new, v_new, sb_bias_col, cache_k, cache_v, *, pages):
    nb, n_pages = page_table.shape
    n_phys, page_size, n_heads, hd = cache_k.shape
    groups = n_pages // pages
    tok = lambda b, g, pt: (b, 0, 0)
    heads = lambda x: x.reshape(nb, n_heads, hd)

    def page_spec(j):
        return pl.BlockSpec((1, page_size, n_heads, hd),
                            lambda b, g, pt: (pt[b * n_pages + (groups - 1 - g) * pages + j], 0, 0, 0))

    grid_spec = pltpu.PrefetchScalarGridSpec(
        num_scalar_prefetch=1,
        grid=(nb, groups),
        in_specs=[
            pl.BlockSpec((1, n_heads, hd), tok),
            pl.BlockSpec((1, n_heads, hd), tok),
            pl.BlockSpec((1, n_heads, hd), tok),
            pl.BlockSpec((n_heads, 1), lambda b, g, pt: (0, 0)),
        ] + [page_spec(j) for j in range(pages)] * 2,
        out_specs=pl.BlockSpec((1, n_heads, hd), tok),
        scratch_shapes=[pltpu.VMEM((n_heads, 1), F32), pltpu.VMEM((n_heads, hd), F32)],
    )
    out = pl.pallas_call(
        functools.partial(_sb_sample_kernel, pages=pages, past=n_pages * page_size),
        grid_spec=grid_spec,
        out_shape=jax.ShapeDtypeStruct((nb, n_heads, hd), BF16),
        compiler_params=_params("parallel", "arbitrary"),
        name="sb_sample",
    )(page_table.reshape(-1), heads(q), heads(k_new), heads(v_new),
      sb_bias_col, *([cache_k] * pages), *([cache_v] * pages))
    return out.reshape(nb, n_heads * hd)


def _post_mix_kernel(h_ref, a_ref, b_ref, wo_ref, g_ref, bb_ref, o_ref, *, alpha):
    aw = a_ref.shape[1]
    mix = _dot(a_ref[...], wo_ref[:aw, :]) + _dot(b_ref[...], wo_ref[aw:, :])
    o_ref[...] = _layer_norm(alpha * h_ref[...] + mix, g_ref[...], bb_ref[...])


def _post_mix(h, a, b, w_o, g, bb, *, alpha, tm):
    m, d = h.shape
    row = lambda width: pl.BlockSpec((tm, width), lambda i: (i, 0))
    const = lambda shape: pl.BlockSpec(shape, lambda i: (0, 0))
    return pl.pallas_call(
        functools.partial(_post_mix_kernel, alpha=alpha),
        grid=(m // tm,),
        in_specs=[row(d), row(a.shape[1]), row(b.shape[1]), const(w_o.shape), const((1, d)), const((1, d))],
        out_specs=row(d),
        out_shape=jax.ShapeDtypeStruct((m, d), F32),
        compiler_params=_params("parallel"),
        name="post_mix",
    )(h, a, b, w_o, g, bb)


def kernel(x_prompt, x_sample, cache_k, cache_v, page_table, ln1_g, ln1_b, ffn1_w_gate, ffn1_w_up, ffn1_w_down, w_in, gmlp_ln_g, gmlp_ln_b, gmlp_w_s, gmlp_b_s, sb_bias, w_o, ln2_g, ln2_b, ffn2_w_gate, ffn2_w_up, ffn2_w_down, ln3_g, ln3_b):
    depth = w_in.shape[0]
    alpha = (2 * depth) ** 0.25
    n, t, d = x_prompt.shape
    nb, t_new, _ = x_sample.shape
    assert t_new == 1, "the sample path handles one new position per sequence"
    n_heads, hd = cache_k.shape[3], cache_k.shape[4]
    assert hd == HEAD_DIM
    bw = n_heads * hd
    aw = gmlp_ln_g.shape[1]
    n_groups = aw // A_GROUP
    row = lambda p: p.reshape(1, -1)

    xp = x_prompt.reshape(n * t, d)
    xs = x_sample.reshape(nb * t_new, d)
    kp_l, vp_l, ks_l, vs_l, cv_l = [], [], [], [], []
    for l in range(depth):
        bf = lambda p: p[l].astype(BF16)
        wg1, wu1, wd1 = bf(ffn1_w_gate), bf(ffn1_w_up), bf(ffn1_w_down)
        wg2, wu2, wd2 = bf(ffn2_w_gate), bf(ffn2_w_up), bf(ffn2_w_down)
        win, wo = bf(w_in), bf(w_o)
        ffn1 = functools.partial(_ffn_ln, wg=wg1, wu=wu1, wd=wd1, g=row(ln1_g[l]), b=row(ln1_b[l]), alpha=alpha)
        ffn2 = functools.partial(_ffn_ln, wg=wg2, wu=wu2, wd=wd2, g=row(ln3_g[l]), b=row(ln3_b[l]), alpha=alpha)
        proj = functools.partial(_in_proj, w_in=win, lg=row(gmlp_ln_g[l]), lb=row(gmlp_ln_b[l]))
        mix = functools.partial(_post_mix, w_o=wo, g=row(ln2_g[l]), bb=row(ln2_b[l]), alpha=alpha)

        hp = ffn1(xp, tm=512, tf=512)
        u, v, q, k32, vb32, k16, vb16 = proj(hp, tm=512, v_dtype=BF16)
        a_out = _gate(u, v, gmlp_w_s[l], gmlp_b_s[l].T, chunks=4)
        b_out = _sb_prompt(q, k16, vb16, sb_bias[l], n=n, t=t, heads=2, tq=256)
        xp = ffn2(mix(hp, a_out, b_out, tm=512), tm=512, tf=512)
        kp_l.append(k32.reshape(n, t, n_heads, hd))
        vp_l.append(vb32.reshape(n, t, n_heads, hd))

        hs = ffn1(xs, tm=nb, tf=512)
        u_s, v_s, q_s, k_s, vb_s, _, _ = proj(hs, tm=nb, v_dtype=F32)
        w0 = jnp.repeat(gmlp_w_s[l][:, 0, 0], A_GROUP).reshape(1, aw)
        b0 = jnp.repeat(gmlp_b_s[l][:, 0], A_GROUP).reshape(1, aw)
        a_s = _sample_gate(u_s, v_s, w0, b0)
        b_s = _sb_sample(page_table, q_s, k_s, vb_s, sb_bias[l].reshape(n_heads, 1),
                         cache_k[l], cache_v[l], pages=8)
        xs = ffn2(mix(hs, a_s, b_s, tm=nb), tm=nb, tf=512)
        ks_l.append(k_s.reshape(nb, t_new, n_heads, hd))
        vs_l.append(vb_s.reshape(nb, t_new, n_heads, hd))
        cv_l.append(v_s.reshape(nb, t_new, aw))

    return (xp.reshape(n, t, d), xs.reshape(nb, t_new, d),
            jnp.stack(kp_l), jnp.stack(vp_l), jnp.stack(ks_l), jnp.stack(vs_l), jnp.stack(cv_l))
```

```python
import functools

import jax
import jax.numpy as jnp
from jax import lax
from jax.experimental import pallas as pl
from jax.experimental.pallas import tpu as pltpu

LN_EPS = 1e-5
HEAD_DIM = 128
A_GROUP = 128
CHUNK = 128

V7X_VMEM_LIMIT_BYTES = 56 * 1024 * 1024
V7X_LANES = 128

F32 = jnp.float32
BF16 = jnp.bfloat16
LOG2E = 1.4426950408889634


def _params(*semantics):
    return pltpu.CompilerParams(dimension_semantics=semantics,
                                vmem_limit_bytes=V7X_VMEM_LIMIT_BYTES)


def _layer_norm(y, g, b):
    mu = jnp.mean(y, axis=-1, keepdims=True)
    d = y - mu
    var = jnp.mean(d * d, axis=-1, keepdims=True)
    return d * lax.rsqrt(var + LN_EPS) * g + b


def _dot(a, b):
    return jnp.dot(a, b, preferred_element_type=F32)


def _dot_nt(a, b):
    return lax.dot_general(a, b, (((1,), (1,)), ((), ())), preferred_element_type=F32)


def _stick_logs2(z2):
    sp = jnp.log(1.0 + jnp.exp2(-jnp.abs(z2))) * LOG2E
    lb = jnp.minimum(z2, 0.0) - sp
    return lb, lb - z2


def _split_bf16(x):
    hi = x.astype(BF16)
    lo = (x - hi.astype(F32)).astype(BF16)
    return hi, lo


def _strict_after_matrix(n):
    j = lax.broadcasted_iota(jnp.int32, (n, n), 0)
    s = lax.broadcasted_iota(jnp.int32, (n, n), 1)
    return (j > s).astype(BF16)


def _ffn_ln_kernel(x_ref, wg_ref, wu_ref, wd_ref, g_ref, b_ref, o_ref, xb_ref, *, alpha):
    j = pl.program_id(1)

    @pl.when(j == 0)
    def _():
        xb_ref[...] = x_ref[...].astype(BF16)
        o_ref[...] = jnp.zeros_like(o_ref)

    xb = xb_ref[...]
    gate = _dot(xb, wg_ref[...])
    up = _dot(xb, wu_ref[...])
    h = (jax.nn.silu(gate) * up).astype(BF16)
    o_ref[...] += _dot(h, wd_ref[...])

    @pl.when(j == pl.num_programs(1) - 1)
    def _():
        y = alpha * x_ref[...] + 0.5 * o_ref[...]
        o_ref[...] = _layer_norm(y, g_ref[...], b_ref[...])


def _ffn_ln(x, wg, wu, wd, g, b, *, alpha, tm, tf):
    m, d = x.shape
    d_ff = wg.shape[1]
    return pl.pallas_call(
        functools.partial(_ffn_ln_kernel, alpha=alpha),
        grid=(m // tm, d_ff // tf),
        in_specs=[
            pl.BlockSpec((tm, d), lambda i, j: (i, 0)),
            pl.BlockSpec((d, tf), lambda i, j: (0, j)),
            pl.BlockSpec((d, tf), lambda i, j: (0, j)),
            pl.BlockSpec((tf, d), lambda i, j: (j, 0)),
            pl.BlockSpec((1, d), lambda i, j: (0, 0)),
            pl.BlockSpec((1, d), lambda i, j: (0, 0)),
        ],
        out_specs=pl.BlockSpec((tm, d), lambda i, j: (i, 0)),
        out_shape=jax.ShapeDtypeStruct((m, d), F32),
        scratch_shapes=[pltpu.VMEM((tm, d), BF16)],
        compiler_params=_params("parallel", "arbitrary"),
        name="ffn_ln",
    )(x, wg, wu, wd, g, b)


def _in_proj_kernel(h_ref, w_ref, lg_ref, lb_ref,
                    u_ref, v_ref, q_ref, k32_ref, vb32_ref, k16_ref, vb16_ref,
                    hb_ref, *, q_scale):
    c = pl.program_id(1)

    @pl.when(c == 0)
    def _():
        hb_ref[...] = h_ref[...].astype(BF16)

    z = _dot(hb_ref[...], w_ref[...])

    @pl.when(c == 0)
    def _():
        u_ref[...] = jax.nn.gelu(z)

    @pl.when(c == 1)
    def _():
        v_ref[...] = _layer_norm(jax.nn.gelu(z), lg_ref[...], lb_ref[...]).astype(v_ref.dtype)

    @pl.when(c == 2)
    def _():
        q_ref[...] = (z * q_scale).astype(BF16)

    @pl.when(c == 3)
    def _():
        k32_ref[...] = z
        k16_ref[...] = z.astype(BF16)

    @pl.when(c == 4)
    def _():
        vb32_ref[...] = z
        vb16_ref[...] = z.astype(BF16)


def _in_proj(h, w_in, lg, lb, *, tm, v_dtype):
    m, d = h.shape
    w = w_in.shape[1] // 5
    row = lambda i, c: (i, 0)
    out = lambda dt: jax.ShapeDtypeStruct((m, w), dt)
    return pl.pallas_call(
        functools.partial(_in_proj_kernel, q_scale=HEAD_DIM ** -0.5 * LOG2E),
        grid=(m // tm, 5),
        in_specs=[
            pl.BlockSpec((tm, d), row),
            pl.BlockSpec((d, w), lambda i, c: (0, c)),
            pl.BlockSpec((1, w), lambda i, c: (0, 0)),
            pl.BlockSpec((1, w), lambda i, c: (0, 0)),
        ],
        out_specs=[pl.BlockSpec((tm, w), row)] * 7,
        out_shape=[out(F32), out(v_dtype), out(BF16), out(F32), out(F32), out(BF16), out(BF16)],
        scratch_shapes=[pltpu.VMEM((tm, d), BF16)],
        compiler_params=_params("parallel", "arbitrary"),
        name="in_proj",
    )(h, w_in, lg, lb)


def _gate_kernel(u_ref, v_ref, ws_ref, bst_ref, a_ref, *, chunks):
    n_groups = ws_ref.shape[0]
    t = lax.broadcasted_iota(jnp.int32, (CHUNK, CHUNK), 0)
    s = lax.broadcasted_iota(jnp.int32, (CHUNK, CHUNK), 1)
    causal = s <= t
    for g in range(n_groups):
        w = jnp.where(causal, ws_ref[g], 0.0).astype(BF16)
        bias = bst_ref[:, g:g + 1]
        cols = slice(g * A_GROUP, (g + 1) * A_GROUP)
        for c in range(chunks):
            rows = slice(c * CHUNK, (c + 1) * CHUNK)
            mixed = _dot(w, v_ref[rows, cols]) + bias
            a_ref[rows, cols] = (u_ref[rows, cols] * mixed).astype(a_ref.dtype)


def _gate(u, v, w_s, b_s_t, *, chunks):
    m, w = u.shape
    rows = chunks * CHUNK
    return pl.pallas_call(
        functools.partial(_gate_kernel, chunks=chunks),
        grid=(m // rows,),
        in_specs=[
            pl.BlockSpec((rows, w), lambda i: (i, 0)),
            pl.BlockSpec((rows, w), lambda i: (i, 0)),
            pl.BlockSpec(w_s.shape, lambda i: (0, 0, 0)),
            pl.BlockSpec(b_s_t.shape, lambda i: (0, 0)),
        ],
        out_specs=pl.BlockSpec((rows, w), lambda i: (i, 0)),
        out_shape=jax.ShapeDtypeStruct((m, w), BF16),
        compiler_params=_params("parallel"),
        name="gate",
    )(u, v, w_s, b_s_t)


def _sample_gate_kernel(u_ref, v_ref, w0_ref, b0_ref, a_ref):
    a_ref[...] = (u_ref[...] * (w0_ref[...] * v_ref[...] + b0_ref[...])).astype(a_ref.dtype)


def _sample_gate(u, v, w0, b0):
    m, w = u.shape
    full = lambda shape: pl.BlockSpec(shape, lambda i: (0, 0))
    return pl.pallas_call(
        _sample_gate_kernel,
        grid=(1,),
        in_specs=[full((m, w)), full((m, w)), full((1, w)), full((1, w))],
        out_specs=full((m, w)),
        out_shape=jax.ShapeDtypeStruct((m, w), BF16),
        compiler_params=_params("arbitrary"),
        name="sample_gate",
    )(u, v, w0, b0)


def _sb_prompt_kernel(bias_ref, q_ref, k_ref, v_ref, o_ref, *, heads, tq):
    hg = pl.program_id(1)
    qi = pl.program_id(2)
    after = _strict_after_matrix(tq)
    t = lax.broadcasted_iota(jnp.int32, (tq, tq), 0)
    s = lax.broadcasted_iota(jnp.int32, (tq, tq), 1)
    diag_mask = s < t

    cols = [slice(hh * HEAD_DIM, (hh + 1) * HEAD_DIM) for hh in range(heads)]
    bias2 = [bias_ref[hg * heads + hh] * LOG2E for hh in range(heads)]

    def tile(kb, carry, masked):
        start = pl.multiple_of(kb * tq, tq)
        rows = pl.ds(start, tq)
        z2 = [_dot_nt(q_ref[:, c], k_ref[rows, c]) + b for c, b in zip(cols, bias2)]
        lbs, splits, sums = [], [], []
        for z in z2:
            lb, l1m = _stick_logs2(z)
            if masked:
                l1m = jnp.where(diag_mask, l1m, 0.0)
            lbs.append(lb)
            splits.append(_split_bf16(l1m))
            sums.append(jnp.sum(l1m, axis=1, keepdims=True))
        afts = [_dot(hi, after) + _dot(lo, after) for hi, lo in splits]
        probs = []
        for lb, aft, (run, _) in zip(lbs, afts, carry):
            a = jnp.exp2(lb + aft + run)
            if masked:
                a = jnp.where(diag_mask, a, 0.0)
            probs.append(a.astype(BF16))
        accs = [acc + _dot(a, v_ref[rows, c]) for a, c, (_, acc) in zip(probs, cols, carry)]
        return tuple((run + s, acc) for (run, _), s, acc in zip(carry, sums, accs))

    init = tuple((jnp.zeros((tq, 1), F32), jnp.zeros((tq, HEAD_DIM), F32)) for _ in range(heads))
    carry = tile(qi, init, True)
    carry = lax.fori_loop(0, qi, lambda i, c: tile(qi - 1 - i, c, False), carry)
    for hh in range(heads):
        o_ref[:, hh * HEAD_DIM:(hh + 1) * HEAD_DIM] = carry[hh][1].astype(o_ref.dtype)


def _sb_prompt(q, k, v, sb_bias, *, n, t, heads, tq):
    m, w = q.shape
    gw = heads * HEAD_DIM
    nq = t // tq
    return pl.pallas_call(
        functools.partial(_sb_prompt_kernel, heads=heads, tq=tq),
        grid=(n, w // gw, nq),
        in_specs=[
            pl.BlockSpec(memory_space=pltpu.SMEM),
            pl.BlockSpec((tq, gw), lambda b, h, i: (b * nq + i, h)),
            pl.BlockSpec((t, gw), lambda b, h, i: (b, h)),
            pl.BlockSpec((t, gw), lambda b, h, i: (b, h)),
        ],
        out_specs=pl.BlockSpec((tq, gw), lambda b, h, i: (b * nq + i, h)),
        out_shape=jax.ShapeDtypeStruct((m, w), BF16),
        compiler_params=_params("parallel", "parallel", "arbitrary"),
        name="sb_prompt",
    )(sb_bias, q, k, v)


def _sum_over_rows(x):
    n = x.shape[0]
    shift = n // 2
    while shift:
        x = x + pltpu.roll(x, shift, 0)
        shift //= 2
    return x


def _sum_of_later_rows(x, row):
    n = x.shape[0]
    up = lambda y, s: jnp.where(row < n - s, pltpu.roll(y, n - s, 0), 0.0)
    y = up(x, 1)
    shift = 1
    while shift < n:
        y = y + up(y, shift)
        shift *= 2
    return y


def _sb_sample_kernel(pt_ref, q_ref, kn_ref, vn_ref, bias_col_ref, bias_lane_ref, *refs, pages, past):
    del pt_ref
    k_refs = refs[:pages]
    v_refs = refs[pages:2 * pages]
    o_ref, run_ref, acc_ref = refs[2 * pages:]
    g = pl.program_id(1)
    n_heads, hd = acc_ref.shape
    n_row, lanes = run_ref.shape
    page_size = k_refs[0].shape[0] // n_heads
    per_row = lanes // n_heads
    q_pos = past
    q = q_ref[0]

    row = lax.broadcasted_iota(jnp.int32, (n_row, lanes), 0)
    lane = lax.broadcasted_iota(jnp.int32, (n_row, lanes), 1)
    own = (lax.broadcasted_iota(jnp.int32, (n_heads, lanes), 1) % n_heads
           == lax.broadcasted_iota(jnp.int32, (n_heads, lanes), 0))

    @pl.when(g == 0)
    def _():
        z = jnp.sum(q.astype(F32) * kn_ref[0], axis=1, keepdims=True) + bias_col_ref[...] * LOG2E
        lb, l1m = _stick_logs2(z)
        seen = jnp.full(z.shape, past, jnp.int32) < q_pos
        l1m = jnp.where(seen, l1m, 0.0)
        run_ref[...] = _sum_over_rows(jnp.where(own, l1m, 0.0))
        acc_ref[...] = jnp.where(seen, jnp.exp2(lb), 0.0) * vn_ref[0]

    src = lax.broadcasted_iota(jnp.int32, (lanes, lanes), 0)
    dst = lax.broadcasted_iota(jnp.int32, (lanes, lanes), 1)
    same_head = src % n_heads == dst % n_heads
    scan = jnp.concatenate([(same_head & (src > dst)).astype(BF16), same_head.astype(BF16)], axis=1)

    bias_row = bias_lane_ref[...] * LOG2E
    first_page = (pl.num_programs(1) - 1 - g) * pages
    run = run_ref[...]
    acc = acc_ref[...]
    scores = [_dot_nt(q, k_ref[...].astype(BF16)) for k_ref in k_refs]
    lbs, seens, sums = [], [], []
    for j in range(pages):
        z = jnp.zeros((n_row, lanes), F32)
        for i in range(n_row):
            mine = jnp.where(own, scores[j][:, i * lanes:(i + 1) * lanes], 0.0)
            z = jnp.where(row == i, _sum_over_rows(mine), z)
        lb, l1m = _stick_logs2(z + bias_row)
        k_pos = (first_page + j) * page_size + row * per_row + lane // n_heads
        seen = k_pos < q_pos
        hi, lo = _split_bf16(jnp.where(seen, l1m, 0.0))
        both = _dot(jnp.concatenate([hi, lo], axis=0), scan)
        lbs.append(lb)
        seens.append(seen)
        sums.append(both[:n_row] + both[n_row:])
    wides = [None] * pages
    for j in reversed(range(pages)):
        in_row, row_total = sums[j][:, :lanes], sums[j][:, lanes:]
        later_rows = _sum_of_later_rows(row_total, row)
        a = jnp.where(seens[j], jnp.exp2(lbs[j] + in_row + later_rows + run), 0.0)
        run = run + jnp.broadcast_to(later_rows[0:1] + row_total[0:1], run.shape)
        wides[j] = jnp.concatenate(
            [jnp.where(own, jnp.broadcast_to(a[i:i + 1], (n_heads, lanes)), 0.0) for i in range(n_row)],
            axis=1).astype(BF16)
    for wide, v_ref in zip(wides, v_refs):
        acc = acc + _dot(wide, v_ref[...].astype(BF16))
    run_ref[...] = run
    acc_ref[...] = acc

    @pl.when(g == pl.num_programs(1) - 1)
    def _():
        o_ref[0] = acc.astype(o_ref.dtype)


def _sb_sample(page_table, q, k_new, v_new, sb_bias, cache_k, cache_v, *, pages):
    nb, n_pages = page_table.shape
    n_phys, page_size, n_heads, hd = cache_k.shape
    page_rows = page_size * n_heads
    assert page_rows % V7X_LANES == 0 and V7X_LANES % n_heads == 0 and n_pages % pages == 0
    n_row = page_rows // V7X_LANES
    assert n_row == n_heads, "the per-head running mass shares the (n_row, lanes) tile shape"
    groups = n_pages // pages
    tok = lambda b, g, pt: (b, 0, 0)
    const = lambda b, g, pt: (0, 0)
    heads = lambda x: x.reshape(nb, n_heads, hd)
    ck = cache_k.reshape(n_phys * page_rows, hd)
    cv = cache_v.reshape(n_phys * page_rows, hd)

    def page_spec(j):
        return pl.BlockSpec((page_rows, hd),
                            lambda b, g, pt: (pt[b * n_pages + (groups - 1 - g) * pages + j], 0))

    grid_spec = pltpu.PrefetchScalarGridSpec(
        num_scalar_prefetch=1,
        grid=(nb, groups),
        in_specs=[
            pl.BlockSpec((1, n_heads, hd), tok),
            pl.BlockSpec((1, n_heads, hd), tok),
            pl.BlockSpec((1, n_heads, hd), tok),
            pl.BlockSpec((n_heads, 1), const),
            pl.BlockSpec((1, V7X_LANES), const),
        ] + [page_spec(j) for j in range(pages)] * 2,
        out_specs=pl.BlockSpec((1, n_heads, hd), tok),
        scratch_shapes=[pltpu.VMEM((n_row, V7X_LANES), F32), pltpu.VMEM((n_heads, hd), F32)],
    )
    out = pl.pallas_call(
        functools.partial(_sb_sample_kernel, pages=pages, past=n_pages * page_size),
        grid_spec=grid_spec,
        out_shape=jax.ShapeDtypeStruct((nb, n_heads, hd), BF16),
        compiler_params=_params("parallel", "arbitrary"),
        name="sb_sample",
    )(page_table.reshape(-1), heads(q), heads(k_new), heads(v_new),
      sb_bias.reshape(n_heads, 1), jnp.tile(sb_bias, V7X_LANES // n_heads).reshape(1, V7X_LANES),
      *([ck] * pages), *([cv] * pages))
    return out.reshape(nb, n_heads * hd)


def _post_mix_kernel(h_ref, a_ref, b_ref, wo_ref, g_ref, bb_ref, o_ref, *, alpha):
    aw = a_ref.shape[1]
    mix = _dot(a_ref[...], wo_ref[:aw, :]) + _dot(b_ref[...], wo_ref[aw:, :])
    o_ref[...] = _layer_norm(alpha * h_ref[...] + mix, g_ref[...], bb_ref[...])


def _post_mix(h, a, b, w_o, g, bb, *, alpha, tm):
    m, d = h.shape
    row = lambda width: pl.BlockSpec((tm, width), lambda i: (i, 0))
    const = lambda shape: pl.BlockSpec(shape, lambda i: (0, 0))
    return pl.pallas_call(
        functools.partial(_post_mix_kernel, alpha=alpha),
        grid=(m // tm,),
        in_specs=[row(d), row(a.shape[1]), row(b.shape[1]), const(w_o.shape), const((1, d)), const((1, d))],
        out_specs=row(d),
        out_shape=jax.ShapeDtypeStruct((m, d), F32),
        compiler_params=_params("parallel"),
        name="post_mix",
    )(h, a, b, w_o, g, bb)


def kernel(x_prompt, x_sample, cache_k, cache_v, page_table, ln1_g, ln1_b, ffn1_w_gate, ffn1_w_up, ffn1_w_down, w_in, gmlp_ln_g, gmlp_ln_b, gmlp_w_s, gmlp_b_s, sb_bias, w_o, ln2_g, ln2_b, ffn2_w_gate, ffn2_w_up, ffn2_w_down, ln3_g, ln3_b):
    depth = w_in.shape[0]
    alpha = (2 * depth) ** 0.25
    n, t, d = x_prompt.shape
    nb, t_new, _ = x_sample.shape
    assert t_new == 1, "the sample path handles one new position per sequence"
    n_heads, hd = cache_k.shape[3], cache_k.shape[4]
    assert hd == HEAD_DIM
    bw = n_heads * hd
    aw = gmlp_ln_g.shape[1]
    n_groups = aw // A_GROUP
    row = lambda p: p.reshape(1, -1)

    xp = x_prompt.reshape(n * t, d)
    xs = x_sample.reshape(nb * t_new, d)
    kp_l, vp_l, ks_l, vs_l, cv_l = [], [], [], [], []
    for l in range(depth):
        bf = lambda p: p[l].astype(BF16)
        wg1, wu1, wd1 = bf(ffn1_w_gate), bf(ffn1_w_up), bf(ffn1_w_down)
        wg2, wu2, wd2 = bf(ffn2_w_gate), bf(ffn2_w_up), bf(ffn2_w_down)
        win, wo = bf(w_in), bf(w_o)
        ffn1 = functools.partial(_ffn_ln, wg=wg1, wu=wu1, wd=wd1, g=row(ln1_g[l]), b=row(ln1_b[l]), alpha=alpha)
        ffn2 = functools.partial(_ffn_ln, wg=wg2, wu=wu2, wd=wd2, g=row(ln3_g[l]), b=row(ln3_b[l]), alpha=alpha)
        proj = functools.partial(_in_proj, w_in=win, lg=row(gmlp_ln_g[l]), lb=row(gmlp_ln_b[l]))
        mix = functools.partial(_post_mix, w_o=wo, g=row(ln2_g[l]), bb=row(ln2_b[l]), alpha=alpha)

        hp = ffn1(xp, tm=512, tf=512)
        u, v, q, k32, vb32, k16, vb16 = proj(hp, tm=512, v_dtype=BF16)
        a_out = _gate(u, v, gmlp_w_s[l], gmlp_b_s[l].T, chunks=4)
        b_out = _sb_prompt(q, k16, vb16, sb_bias[l], n=n, t=t, heads=4, tq=256)
        xp = ffn2(mix(hp, a_out, b_out, tm=512), tm=512, tf=512)
        kp_l.append(k32.reshape(n, t, n_heads, hd))
        vp_l.append(vb32.reshape(n, t, n_heads, hd))

        hs = ffn1(xs, tm=nb, tf=512)
        u_s, v_s, q_s, k_s, vb_s, _, _ = proj(hs, tm=nb, v_dtype=F32)
        w0 = jnp.repeat(gmlp_w_s[l][:, 0, 0], A_GROUP).reshape(1, aw)
        b0 = jnp.repeat(gmlp_b_s[l][:, 0], A_GROUP).reshape(1, aw)
        a_s = _sample_gate(u_s, v_s, w0, b0)
        b_s = _sb_sample(page_table, q_s, k_s, vb_s, sb_bias[l], cache_k[l], cache_v[l], pages=16)
        xs = ffn2(mix(hs, a_s, b_s, tm=nb), tm=nb, tf=512)
        ks_l.append(k_s.reshape(nb, t_new, n_heads, hd))
        vs_l.append(vb_s.reshape(nb, t_new, n_heads, hd))
        cv_l.append(v_s.reshape(nb, t_new, aw))

    return (xp.reshape(n, t, d), xs.reshape(nb, t_new, d),
            jnp.stack(kp_l), jnp.stack(vp_l), jnp.stack(ks_l), jnp.stack(vs_l), jnp.stack(cv_l))
```

```python
import functools

import jax
import jax.numpy as jnp
from jax import lax
from jax.experimental import pallas as pl
from jax.experimental.pallas import tpu as pltpu

LN_EPS = 1e-5
HEAD_DIM = 128
A_GROUP = 128
CHUNK = 128

V7X_VMEM_LIMIT_BYTES = 60000 * 1024
V7X_LANES = 128

F32 = jnp.float32
BF16 = jnp.bfloat16
LOG2E = 1.4426950408889634
IN_PROJ_CHUNK = 256
SB_STRIP = 32
MASKED_LOG2 = -1e30


def _params(*semantics):
    return pltpu.CompilerParams(dimension_semantics=semantics,
                                vmem_limit_bytes=V7X_VMEM_LIMIT_BYTES)


def _layer_norm(y, g, b):
    mu = jnp.mean(y, axis=-1, keepdims=True)
    d = y - mu
    var = jnp.mean(d * d, axis=-1, keepdims=True)
    return d * lax.rsqrt(var + LN_EPS) * g + b


def _dot(a, b):
    return jnp.dot(a, b, preferred_element_type=F32)


def _dot_nt(a, b):
    return lax.dot_general(a, b, (((1,), (1,)), ((), ())), preferred_element_type=F32)


def _stick_logs2(z2):
    sp = jnp.log(1.0 + jnp.exp2(-jnp.abs(z2))) * LOG2E
    lb = jnp.minimum(z2, 0.0) - sp
    return lb, lb - z2


def _split_bf16(x):
    hi = x.astype(BF16)
    lo = (x - hi.astype(F32)).astype(BF16)
    return hi, lo


def _strict_after_matrix(n):
    j = lax.broadcasted_iota(jnp.int32, (n, n), 0)
    s = lax.broadcasted_iota(jnp.int32, (n, n), 1)
    return (j > s).astype(BF16)


def _ffn_ln_kernel(x_ref, xs_ref, wg_ref, wu_ref, wd_ref, g_ref, b_ref, o_ref, os_ref, xb_ref, xsb_ref, *, alpha):
    i = pl.program_id(0)
    j = pl.program_id(1)

    def rows(x_ref, o_ref, xb_ref):
        @pl.when(j == 0)
        def _():
            xb_ref[...] = x_ref[...].astype(BF16)
            o_ref[...] = jnp.zeros_like(o_ref)

        xb = xb_ref[...]
        gate = _dot(xb, wg_ref[...].astype(BF16))
        up = _dot(xb, wu_ref[...].astype(BF16))
        h = (jax.nn.silu(gate) * up).astype(BF16)
        o_ref[...] += _dot(h, wd_ref[...].astype(BF16))

        @pl.when(j == pl.num_programs(1) - 1)
        def _():
            y = alpha * x_ref[...] + 0.5 * o_ref[...]
            o_ref[...] = _layer_norm(y, g_ref[...], b_ref[...])

    rows(x_ref, o_ref, xb_ref)

    @pl.when(i == pl.num_programs(0) - 1)
    def _():
        rows(xs_ref, os_ref, xsb_ref)


def _ffn_ln(x, xs, wg, wu, wd, g, b, *, alpha, tm, tf):
    m, d = x.shape
    ms = xs.shape[0]
    d_ff = wg.shape[1]
    once = pl.Buffered(1)
    return pl.pallas_call(
        functools.partial(_ffn_ln_kernel, alpha=alpha),
        grid=(m // tm, d_ff // tf),
        in_specs=[
            pl.BlockSpec((tm, d), lambda i, j: (i, 0), pipeline_mode=once),
            pl.BlockSpec((ms, d), lambda i, j: (0, 0), pipeline_mode=once),
            pl.BlockSpec((d, tf), lambda i, j: (0, j)),
            pl.BlockSpec((d, tf), lambda i, j: (0, j)),
            pl.BlockSpec((tf, d), lambda i, j: (j, 0)),
            pl.BlockSpec((1, d), lambda i, j: (0, 0)),
            pl.BlockSpec((1, d), lambda i, j: (0, 0)),
        ],
        out_specs=[pl.BlockSpec((tm, d), lambda i, j: (i, 0)),
                   pl.BlockSpec((ms, d), lambda i, j: (0, 0))],
        out_shape=[jax.ShapeDtypeStruct((m, d), F32), jax.ShapeDtypeStruct((ms, d), F32)],
        scratch_shapes=[pltpu.VMEM((tm, d), BF16), pltpu.VMEM((ms, d), BF16)],
        compiler_params=_params("arbitrary", "arbitrary"),
        name="ffn_ln",
    )(x, xs, wg, wu, wd, g, b)


def _in_proj_kernel(h_ref, w_ref, lg_ref, lb_ref,
                    u_ref, v_ref, q_ref, k32_ref, vb32_ref, k16_ref, vb16_ref,
                    hb_ref, gelu_ref, *, q_scale):
    c = pl.program_id(1)
    width = w_ref.shape[1]
    chunks = [slice(n0, n0 + IN_PROJ_CHUNK) for n0 in range(0, width, IN_PROJ_CHUNK)]

    @pl.when(c == 0)
    def _():
        hb_ref[...] = h_ref[...].astype(BF16)

    def project(cols):
        return _dot(hb_ref[...], w_ref[:, cols].astype(BF16))

    @pl.when(c == 0)
    def _():
        for cols in chunks:
            u_ref[:, cols] = jax.nn.gelu(project(cols))

    @pl.when(c == 1)
    def _():
        total = jnp.zeros((hb_ref.shape[0], 1), F32)
        for cols in chunks:
            act = jax.nn.gelu(project(cols))
            gelu_ref[:, cols] = act
            total = total + jnp.sum(act, axis=1, keepdims=True)
        d = gelu_ref[...] - total * (1.0 / width)
        var = jnp.mean(d * d, axis=-1, keepdims=True)
        v_ref[...] = (d * lax.rsqrt(var + LN_EPS) * lg_ref[...] + lb_ref[...]).astype(v_ref.dtype)

    @pl.when(c == 2)
    def _():
        for cols in chunks:
            q_ref[:, cols] = (project(cols) * q_scale).astype(BF16)

    @pl.when(c == 3)
    def _():
        for cols in chunks:
            z = project(cols)
            k32_ref[:, cols] = z
            k16_ref[:, cols] = z.astype(BF16)

    @pl.when(c == 4)
    def _():
        for cols in chunks:
            z = project(cols)
            vb32_ref[:, cols] = z
            vb16_ref[:, cols] = z.astype(BF16)


def _in_proj(h, w_in, lg, lb, *, tm, v_dtype):
    m, d = h.shape
    w = w_in.shape[1] // 5
    row = lambda i, c: (i, 0)
    out = lambda dt: jax.ShapeDtypeStruct((m, w), dt)
    return pl.pallas_call(
        functools.partial(_in_proj_kernel, q_scale=HEAD_DIM ** -0.5 * LOG2E),
        grid=(m // tm, 5),
        in_specs=[
            pl.BlockSpec((tm, d), row),
            pl.BlockSpec((d, w), lambda i, c: (0, c)),
            pl.BlockSpec((1, w), lambda i, c: (0, 0)),
            pl.BlockSpec((1, w), lambda i, c: (0, 0)),
        ],
        out_specs=[pl.BlockSpec((tm, w), row)] * 7,
        out_shape=[out(F32), out(v_dtype), out(BF16), out(F32), out(F32), out(BF16), out(BF16)],
        scratch_shapes=[pltpu.VMEM((tm, d), BF16), pltpu.VMEM((tm, w), F32)],
        compiler_params=_params("parallel", "arbitrary"),
        name="in_proj",
    )(h, w_in, lg, lb)


def _gate_kernel(u_ref, v_ref, ws_ref, bst_ref, a_ref, *, chunks):
    n_groups = ws_ref.shape[0]
    t = lax.broadcasted_iota(jnp.int32, (CHUNK, CHUNK), 0)
    s = lax.broadcasted_iota(jnp.int32, (CHUNK, CHUNK), 1)
    causal = s <= t
    for g in range(n_groups):
        w = jnp.where(causal, ws_ref[g], 0.0).astype(BF16)
        bias = bst_ref[:, g:g + 1]
        cols = slice(g * A_GROUP, (g + 1) * A_GROUP)
        for c in range(chunks):
            rows = slice(c * CHUNK, (c + 1) * CHUNK)
            mixed = _dot(w, v_ref[rows, cols]) + bias
            a_ref[rows, cols] = (u_ref[rows, cols] * mixed).astype(a_ref.dtype)


def _gate(u, v, w_s, b_s_t, *, chunks):
    m, w = u.shape
    rows = chunks * CHUNK
    return pl.pallas_call(
        functools.partial(_gate_kernel, chunks=chunks),
        grid=(m // rows,),
        in_specs=[
            pl.BlockSpec((rows, w), lambda i: (i, 0)),
            pl.BlockSpec((rows, w), lambda i: (i, 0)),
            pl.BlockSpec(w_s.shape, lambda i: (0, 0, 0)),
            pl.BlockSpec(b_s_t.shape, lambda i: (0, 0)),
        ],
        out_specs=pl.BlockSpec((rows, w), lambda i: (i, 0)),
        out_shape=jax.ShapeDtypeStruct((m, w), BF16),
        compiler_params=_params("parallel"),
        name="gate",
    )(u, v, w_s, b_s_t)


def _sample_gate_kernel(u_ref, v_ref, w0_ref, b0_ref, a_ref):
    a_ref[...] = (u_ref[...] * (w0_ref[...] * v_ref[...] + b0_ref[...])).astype(a_ref.dtype)


def _sample_gate(u, v, w0, b0):
    m, w = u.shape
    full = lambda shape: pl.BlockSpec(shape, lambda i: (0, 0))
    return pl.pallas_call(
        _sample_gate_kernel,
        grid=(1,),
        in_specs=[full((m, w)), full((m, w)), full((1, w)), full((1, w))],
        out_specs=full((m, w)),
        out_shape=jax.ShapeDtypeStruct((m, w), BF16),
        compiler_params=_params("arbitrary"),
        name="sample_gate",
    )(u, v, w0, b0)


def _sb_prompt_kernel(bias_ref, q_ref, k_ref, v_ref, o_ref, lb_ref, add_ref, acc_ref, *, heads, tq):
    hg = pl.program_id(1)
    qi = pl.program_id(2)
    after = _strict_after_matrix(tq)
    t = lax.broadcasted_iota(jnp.int32, (tq, tq), 0)
    s = lax.broadcasted_iota(jnp.int32, (tq, tq), 1)
    diag_mask = s < t

    cols = [slice(hh * HEAD_DIM, (hh + 1) * HEAD_DIM) for hh in range(heads)]
    bias2 = [bias_ref[hg * heads + hh] * LOG2E for hh in range(heads)]

    def key_rows(kb):
        return pl.ds(pl.multiple_of(kb * tq, tq), tq)

    strips = [slice(r, r + SB_STRIP) for r in range(0, tq, SB_STRIP)]

    def logits(kb):
        return [_dot_nt(q_ref[:, c], k_ref[key_rows(kb), c]) for c in cols]

    def stash_logs(z2, slot, masked):
        sums = []
        for hh, (z, b) in enumerate(zip(z2, bias2)):
            parts = []
            for rows in strips:
                lb, l1m = _stick_logs2(z[rows] + b)
                if masked:
                    lb = jnp.where(diag_mask[rows], lb, MASKED_LOG2)
                    l1m = jnp.where(diag_mask[rows], l1m, 0.0)
                lb_ref[slot, hh, rows, :] = lb
                add_ref[slot, hh, rows, :] = l1m.astype(BF16)
                parts.append(jnp.sum(l1m, axis=1, keepdims=True))
            sums.append(jnp.concatenate(parts, axis=0))
        return sums

    def later_mass(slot):
        return [_dot(add_ref[slot, hh], after) for hh in range(heads)]

    def weights(slot, afts, runs):
        return [jnp.concatenate(
                    [jnp.exp2(lb_ref[slot, hh, rows, :] + aft[rows] + run[rows]).astype(BF16) for rows in strips],
                    axis=0)
                for hh, (aft, run) in enumerate(zip(afts, runs))]

    def accumulate(kb, probs):
        for a, c in zip(probs, cols):
            acc_ref[:, c] += _dot(a, v_ref[key_rows(kb), c])

    acc_ref[...] = jnp.zeros_like(acc_ref)
    zero = [jnp.zeros((tq, 1), F32) for _ in range(heads)]
    sums = stash_logs(logits(qi), 0, True)

    def step(i, carry):
        runs, sums = carry
        kb = qi - i
        slot = i % 2
        afts = later_mass(slot)
        z2 = logits(kb - 1)
        accumulate(kb, weights(slot, afts, runs))
        next_sums = stash_logs(z2, 1 - slot, False)
        return [r + s for r, s in zip(runs, sums)], next_sums

    runs, _ = lax.fori_loop(0, qi, step, (zero, sums))
    last = qi % 2
    accumulate(0, weights(last, later_mass(last), runs))
    o_ref[...] = acc_ref[...].astype(o_ref.dtype)


def _sb_prompt(q, k, v, sb_bias, *, n, t, heads, tq):
    m, w = q.shape
    gw = heads * HEAD_DIM
    nq = t // tq
    return pl.pallas_call(
        functools.partial(_sb_prompt_kernel, heads=heads, tq=tq),
        grid=(n, w // gw, nq),
        in_specs=[
            pl.BlockSpec(memory_space=pltpu.SMEM),
            pl.BlockSpec((tq, gw), lambda b, h, i: (b * nq + i, h)),
            pl.BlockSpec((t, gw), lambda b, h, i: (b, h)),
            pl.BlockSpec((t, gw), lambda b, h, i: (b, h)),
        ],
        out_specs=pl.BlockSpec((tq, gw), lambda b, h, i: (b * nq + i, h)),
        out_shape=jax.ShapeDtypeStruct((m, w), BF16),
        scratch_shapes=[
            pltpu.VMEM((2, heads, tq, tq), F32),
            pltpu.VMEM((2, heads, tq, tq), BF16),
            pltpu.VMEM((tq, gw), F32),
        ],
        compiler_params=_params("parallel", "parallel", "arbitrary"),
        name="sb_prompt",
    )(sb_bias, q, k, v)


def _sum_over_rows(x):
    n = x.shape[0]
    shift = n // 2
    while shift:
        x = x + pltpu.roll(x, shift, 0)
        shift //= 2
    return x


def _sum_of_later_rows(x, row):
    n = x.shape[0]
    up = lambda y, s: jnp.where(row < n - s, pltpu.roll(y, n - s, 0), 0.0)
    y = up(x, 1)
    shift = 1
    while shift < n:
        y = y + up(y, shift)
        shift *= 2
    return y


def _sb_sample_kernel(pt_ref, q_ref, kn_ref, vn_ref, bias_col_ref, bias_lane_ref, *refs, pages, past):
    del pt_ref
    k_refs = refs[:pages]
    v_refs = refs[pages:2 * pages]
    o_ref, run_ref, acc_ref = refs[2 * pages:]
    g = pl.program_id(1)
    n_heads, hd = acc_ref.shape
    n_row, lanes = run_ref.shape
    page_size = k_refs[0].shape[0] // n_heads
    per_row = lanes // n_heads
    q_pos = past
    q = q_ref[0]

    row = lax.broadcasted_iota(jnp.int32, (n_row, lanes), 0)
    lane = lax.broadcasted_iota(jnp.int32, (n_row, lanes), 1)
    own = (lax.broadcasted_iota(jnp.int32, (n_heads, lanes), 1) % n_heads
           == lax.broadcasted_iota(jnp.int32, (n_heads, lanes), 0))

    @pl.when(g == 0)
    def _():
        z = jnp.sum(q.astype(F32) * kn_ref[0], axis=1, keepdims=True) + bias_col_ref[...] * LOG2E
        lb, l1m = _stick_logs2(z)
        seen = jnp.full(z.shape, past, jnp.int32) < q_pos
        l1m = jnp.where(seen, l1m, 0.0)
        run_ref[...] = _sum_over_rows(jnp.where(own, l1m, 0.0))
        acc_ref[...] = jnp.where(seen, jnp.exp2(lb), 0.0) * vn_ref[0]

    src = lax.broadcasted_iota(jnp.int32, (lanes, lanes), 0)
    dst = lax.broadcasted_iota(jnp.int32, (lanes, lanes), 1)
    same_head = src % n_heads == dst % n_heads
    scan = jnp.concatenate([(same_head & (src > dst)).astype(BF16), same_head.astype(BF16)], axis=1)

    bias_row = bias_lane_ref[...] * LOG2E
    first_page = (pl.num_programs(1) - 1 - g) * pages
    run = run_ref[...]
    acc = acc_ref[...]
    scores = [_dot_nt(q, k_ref[...].astype(BF16)) for k_ref in k_refs]
    lbs, seens, sums = [], [], []
    for j in range(pages):
        z = jnp.zeros((n_row, lanes), F32)
        for i in range(n_row):
            mine = jnp.where(own, scores[j][:, i * lanes:(i + 1) * lanes], 0.0)
            z = jnp.where(row == i, _sum_over_rows(mine), z)
        lb, l1m = _stick_logs2(z + bias_row)
        k_pos = (first_page + j) * page_size + row * per_row + lane // n_heads
        seen = k_pos < q_pos
        hi, lo = _split_bf16(jnp.where(seen, l1m, 0.0))
        both = _dot(jnp.concatenate([hi, lo], axis=0), scan)
        lbs.append(lb)
        seens.append(seen)
        sums.append(both[:n_row] + both[n_row:])
    wides = [None] * pages
    for j in reversed(range(pages)):
        in_row, row_total = sums[j][:, :lanes], sums[j][:, lanes:]
        later_rows = _sum_of_later_rows(row_total, row)
        a = jnp.where(seens[j], jnp.exp2(lbs[j] + in_row + later_rows + run), 0.0)
        run = run + jnp.broadcast_to(later_rows[0:1] + row_total[0:1], run.shape)
        wides[j] = jnp.concatenate(
            [jnp.where(own, jnp.broadcast_to(a[i:i + 1], (n_heads, lanes)), 0.0) for i in range(n_row)],
            axis=1).astype(BF16)
    for wide, v_ref in zip(wides, v_refs):
        acc = acc + _dot(wide, v_ref[...].astype(BF16))
    run_ref[...] = run
    acc_ref[...] = acc

    @pl.when(g == pl.num_programs(1) - 1)
    def _():
        o_ref[0] = acc.astype(o_ref.dtype)


def _sb_sample(page_table, q, k_new, v_new, sb_bias, cache_k, cache_v, *, pages):
    nb, n_pages = page_table.shape
    n_phys, page_size, n_heads, hd = cache_k.shape
    page_rows = page_size * n_heads
    assert page_rows % V7X_LANES == 0 and V7X_LANES % n_heads == 0 and n_pages % pages == 0
    n_row = page_rows // V7X_LANES
    assert n_row == n_heads, "the per-head running mass shares the (n_row, lanes) tile shape"
    groups = n_pages // pages
    tok = lambda b, g, pt: (b, 0, 0)
    const = lambda b, g, pt: (0, 0)
    heads = lambda x: x.reshape(nb, n_heads, hd)
    ck = cache_k.reshape(n_phys * page_rows, hd)
    cv = cache_v.reshape(n_phys * page_rows, hd)

    def page_spec(j):
        return pl.BlockSpec((page_rows, hd),
                            lambda b, g, pt: (pt[b * n_pages + (groups - 1 - g) * pages + j], 0))

    grid_spec = pltpu.PrefetchScalarGridSpec(
        num_scalar_prefetch=1,
        grid=(nb, groups),
        in_specs=[
            pl.BlockSpec((1, n_heads, hd), tok),
            pl.BlockSpec((1, n_heads, hd), tok),
            pl.BlockSpec((1, n_heads, hd), tok),
            pl.BlockSpec((n_heads, 1), const),
            pl.BlockSpec((1, V7X_LANES), const),
        ] + [page_spec(j) for j in range(pages)] * 2,
        out_specs=pl.BlockSpec((1, n_heads, hd), tok),
        scratch_shapes=[pltpu.VMEM((n_row, V7X_LANES), F32), pltpu.VMEM((n_heads, hd), F32)],
    )
    out = pl.pallas_call(
        functools.partial(_sb_sample_kernel, pages=pages, past=n_pages * page_size),
        grid_spec=grid_spec,
        out_shape=jax.ShapeDtypeStruct((nb, n_heads, hd), BF16),
        compiler_params=_params("parallel", "arbitrary"),
        name="sb_sample",
    )(page_table.reshape(-1), heads(q), heads(k_new), heads(v_new),
      sb_bias.reshape(n_heads, 1), jnp.tile(sb_bias, V7X_LANES // n_heads).reshape(1, V7X_LANES),
      *([ck] * pages), *([cv] * pages))
    return out.reshape(nb, n_heads * hd)


def _post_mix_kernel(h_ref, a_ref, b_ref, wo_ref, g_ref, bb_ref, o_ref, *, alpha):
    aw = a_ref.shape[1]
    mix = _dot(a_ref[...], wo_ref[:aw, :]) + _dot(b_ref[...], wo_ref[aw:, :])
    o_ref[...] = _layer_norm(alpha * h_ref[...] + mix, g_ref[...], bb_ref[...])


def _post_mix(h, a, b, w_o, g, bb, *, alpha, tm):
    m, d = h.shape
    row = lambda width: pl.BlockSpec((tm, width), lambda i: (i, 0))
    const = lambda shape: pl.BlockSpec(shape, lambda i: (0, 0))
    return pl.pallas_call(
        functools.partial(_post_mix_kernel, alpha=alpha),
        grid=(m // tm,),
        in_specs=[row(d), row(a.shape[1]), row(b.shape[1]), const(w_o.shape), const((1, d)), const((1, d))],
        out_specs=row(d),
        out_shape=jax.ShapeDtypeStruct((m, d), F32),
        compiler_params=_params("parallel"),
        name="post_mix",
    )(h, a, b, w_o, g, bb)


def kernel(x_prompt, x_sample, cache_k, cache_v, page_table, ln1_g, ln1_b, ffn1_w_gate, ffn1_w_up, ffn1_w_down, w_in, gmlp_ln_g, gmlp_ln_b, gmlp_w_s, gmlp_b_s, sb_bias, w_o, ln2_g, ln2_b, ffn2_w_gate, ffn2_w_up, ffn2_w_down, ln3_g, ln3_b):
    depth = w_in.shape[0]
    alpha = (2 * depth) ** 0.25
    n, t, d = x_prompt.shape
    nb, t_new, _ = x_sample.shape
    assert t_new == 1, "the sample path handles one new position per sequence"
    n_heads, hd = cache_k.shape[3], cache_k.shape[4]
    assert hd == HEAD_DIM
    bw = n_heads * hd
    aw = gmlp_ln_g.shape[1]
    n_groups = aw // A_GROUP
    row = lambda p: p.reshape(1, -1)

    xp = x_prompt.reshape(n * t, d)
    xs = x_sample.reshape(nb * t_new, d)
    kp_l, vp_l, ks_l, vs_l, cv_l = [], [], [], [], []
    for l in range(depth):
        bf = lambda p: p[l].astype(BF16)
        wg1, wu1, wd1 = ffn1_w_gate[l], ffn1_w_up[l], ffn1_w_down[l]
        wg2, wu2, wd2 = ffn2_w_gate[l], ffn2_w_up[l], ffn2_w_down[l]
        win, wo = w_in[l], bf(w_o)
        ffn1 = functools.partial(_ffn_ln, wg=wg1, wu=wu1, wd=wd1, g=row(ln1_g[l]), b=row(ln1_b[l]), alpha=alpha)
        ffn2 = functools.partial(_ffn_ln, wg=wg2, wu=wu2, wd=wd2, g=row(ln3_g[l]), b=row(ln3_b[l]), alpha=alpha)
        proj = functools.partial(_in_proj, w_in=win, lg=row(gmlp_ln_g[l]), lb=row(gmlp_ln_b[l]))
        mix = functools.partial(_post_mix, w_o=wo, g=row(ln2_g[l]), bb=row(ln2_b[l]), alpha=alpha)

        hp, hs = ffn1(xp, xs, tm=1024, tf=256)

        u, v, q, k32, vb32, k16, vb16 = proj(hp, tm=512, v_dtype=BF16)
        a_out = _gate(u, v, gmlp_w_s[l], gmlp_b_s[l].T, chunks=4)
        b_out = _sb_prompt(q, k16, vb16, sb_bias[l], n=n, t=t, heads=4, tq=256)
        kp_l.append(k32.reshape(n, t, n_heads, hd))
        vp_l.append(vb32.reshape(n, t, n_heads, hd))

        u_s, v_s, q_s, k_s, vb_s, _, _ = proj(hs, tm=nb, v_dtype=F32)
        w0 = jnp.repeat(gmlp_w_s[l][:, 0, 0], A_GROUP).reshape(1, aw)
        b0 = jnp.repeat(gmlp_b_s[l][:, 0], A_GROUP).reshape(1, aw)
        a_s = _sample_gate(u_s, v_s, w0, b0)
        b_s = _sb_sample(page_table, q_s, k_s, vb_s, sb_bias[l], cache_k[l], cache_v[l], pages=16)
        xp, xs = ffn2(mix(hp, a_out, b_out, tm=512), mix(hs, a_s, b_s, tm=nb), tm=1024, tf=256)
        ks_l.append(k_s.reshape(nb, t_new, n_heads, hd))
        vs_l.append(vb_s.reshape(nb, t_new, n_heads, hd))
        cv_l.append(v_s.reshape(nb, t_new, aw))

    return (xp.reshape(n, t, d), xs.reshape(nb, t_new, d),
            jnp.stack(kp_l), jnp.stack(vp_l), jnp.stack(ks_l), jnp.stack(vs_l), jnp.stack(cv_l))
```

```python
import functools

import jax
import jax.numpy as jnp
from jax import lax
from jax.experimental import pallas as pl
from jax.experimental.pallas import tpu as pltpu

LN_EPS = 1e-5
HEAD_DIM = 128
A_GROUP = 128
CHUNK = 128

V7X_VMEM_LIMIT_BYTES = 60000 * 1024
V7X_LANES = 128

F32 = jnp.float32
BF16 = jnp.bfloat16
LOG2E = 1.4426950408889634
IN_PROJ_CHUNK = 256
SB_STRIP = 32
MASKED_LOG2 = -1e30


def _params(*semantics):
    return pltpu.CompilerParams(dimension_semantics=semantics,
                                vmem_limit_bytes=V7X_VMEM_LIMIT_BYTES)


def _layer_norm(y, g, b):
    mu = jnp.mean(y, axis=-1, keepdims=True)
    d = y - mu
    var = jnp.mean(d * d, axis=-1, keepdims=True)
    return d * lax.rsqrt(var + LN_EPS) * g + b


def _dot(a, b):
    return jnp.dot(a, b, preferred_element_type=F32)


def _dot_nt(a, b):
    return lax.dot_general(a, b, (((1,), (1,)), ((), ())), preferred_element_type=F32)


def _stick_logs2(z2):
    sp = jnp.log(1.0 + jnp.exp2(-jnp.abs(z2))) * LOG2E
    lb = jnp.minimum(z2, 0.0) - sp
    return lb, lb - z2


def _split_bf16(x):
    hi = x.astype(BF16)
    lo = (x - hi.astype(F32)).astype(BF16)
    return hi, lo


def _strict_after_matrix(n):
    j = lax.broadcasted_iota(jnp.int32, (n, n), 0)
    s = lax.broadcasted_iota(jnp.int32, (n, n), 1)
    return (j > s).astype(BF16)


def _ffn_ln_kernel(x_ref, xs_ref, wg_ref, wu_ref, wd_ref, g_ref, b_ref, o_ref, os_ref, xb_ref, xsb_ref, *, alpha):
    i = pl.program_id(0)
    j = pl.program_id(1)

    def rows(x_ref, o_ref, xb_ref):
        @pl.when(j == 0)
        def _():
            xb_ref[...] = x_ref[...].astype(BF16)
            o_ref[...] = jnp.zeros_like(o_ref)

        xb = xb_ref[...]
        gate = _dot(xb, wg_ref[...].astype(BF16))
        up = _dot(xb, wu_ref[...].astype(BF16))
        h = (jax.nn.silu(gate) * up).astype(BF16)
        o_ref[...] += _dot(h, wd_ref[...].astype(BF16))

        @pl.when(j == pl.num_programs(1) - 1)
        def _():
            y = alpha * x_ref[...] + 0.5 * o_ref[...]
            o_ref[...] = _layer_norm(y, g_ref[...], b_ref[...])

    rows(x_ref, o_ref, xb_ref)

    @pl.when(i == pl.num_programs(0) - 1)
    def _():
        rows(xs_ref, os_ref, xsb_ref)


def _ffn_ln(x, xs, wg, wu, wd, g, b, *, alpha, tm, tf):
    m, d = x.shape
    ms = xs.shape[0]
    d_ff = wg.shape[1]
    once = pl.Buffered(1)
    return pl.pallas_call(
        functools.partial(_ffn_ln_kernel, alpha=alpha),
        grid=(m // tm, d_ff // tf),
        in_specs=[
            pl.BlockSpec((tm, d), lambda i, j: (i, 0), pipeline_mode=once),
            pl.BlockSpec((ms, d), lambda i, j: (0, 0), pipeline_mode=once),
            pl.BlockSpec((d, tf), lambda i, j: (0, j)),
            pl.BlockSpec((d, tf), lambda i, j: (0, j)),
            pl.BlockSpec((tf, d), lambda i, j: (j, 0)),
            pl.BlockSpec((1, d), lambda i, j: (0, 0)),
            pl.BlockSpec((1, d), lambda i, j: (0, 0)),
        ],
        out_specs=[pl.BlockSpec((tm, d), lambda i, j: (i, 0)),
                   pl.BlockSpec((ms, d), lambda i, j: (0, 0))],
        out_shape=[jax.ShapeDtypeStruct((m, d), F32), jax.ShapeDtypeStruct((ms, d), F32)],
        scratch_shapes=[pltpu.VMEM((tm, d), BF16), pltpu.VMEM((ms, d), BF16)],
        compiler_params=_params("arbitrary", "arbitrary"),
        name="ffn_ln",
    )(x, xs, wg, wu, wd, g, b)


def _in_proj_kernel(h_ref, w_ref, lg_ref, lb_ref,
                    u_ref, v_ref, q_ref, k32_ref, vb32_ref, k16_ref, vb16_ref,
                    hb_ref, gelu_ref, *, q_scale):
    c = pl.program_id(1)
    width = w_ref.shape[1]
    chunks = [slice(n0, n0 + IN_PROJ_CHUNK) for n0 in range(0, width, IN_PROJ_CHUNK)]

    @pl.when(c == 0)
    def _():
        hb_ref[...] = h_ref[...].astype(BF16)

    def project(cols):
        return _dot(hb_ref[...], w_ref[:, cols].astype(BF16))

    @pl.when(c == 0)
    def _():
        for cols in chunks:
            u_ref[:, cols] = jax.nn.gelu(project(cols))

    @pl.when(c == 1)
    def _():
        total = jnp.zeros((hb_ref.shape[0], 1), F32)
        for cols in chunks:
            act = jax.nn.gelu(project(cols))
            gelu_ref[:, cols] = act
            total = total + jnp.sum(act, axis=1, keepdims=True)
        d = gelu_ref[...] - total * (1.0 / width)
        var = jnp.mean(d * d, axis=-1, keepdims=True)
        v_ref[...] = (d * lax.rsqrt(var + LN_EPS) * lg_ref[...] + lb_ref[...]).astype(v_ref.dtype)

    @pl.when(c == 2)
    def _():
        for cols in chunks:
            q_ref[:, cols] = (project(cols) * q_scale).astype(BF16)

    @pl.when(c == 3)
    def _():
        for cols in chunks:
            z = project(cols)
            k32_ref[:, cols] = z
            k16_ref[:, cols] = z.astype(BF16)

    @pl.when(c == 4)
    def _():
        for cols in chunks:
            z = project(cols)
            vb32_ref[:, cols] = z
            vb16_ref[:, cols] = z.astype(BF16)


def _in_proj(h, w_in, lg, lb, *, tm, v_dtype):
    m, d = h.shape
    w = w_in.shape[1] // 5
    row = lambda i, c: (i, 0)
    out = lambda dt: jax.ShapeDtypeStruct((m, w), dt)
    return pl.pallas_call(
        functools.partial(_in_proj_kernel, q_scale=HEAD_DIM ** -0.5 * LOG2E),
        grid=(m // tm, 5),
        in_specs=[
            pl.BlockSpec((tm, d), row),
            pl.BlockSpec((d, w), lambda i, c: (0, c)),
            pl.BlockSpec((1, w), lambda i, c: (0, 0)),
            pl.BlockSpec((1, w), lambda i, c: (0, 0)),
        ],
        out_specs=[pl.BlockSpec((tm, w), row)] * 7,
        out_shape=[out(F32), out(v_dtype), out(BF16), out(F32), out(F32), out(BF16), out(BF16)],
        scratch_shapes=[pltpu.VMEM((tm, d), BF16), pltpu.VMEM((tm, w), F32)],
        compiler_params=_params("parallel", "arbitrary"),
        name="in_proj",
    )(h, w_in, lg, lb)


def _gate_kernel(u_ref, v_ref, ws_ref, bst_ref, a_ref, *, chunks):
    n_groups = ws_ref.shape[0]
    t = lax.broadcasted_iota(jnp.int32, (CHUNK, CHUNK), 0)
    s = lax.broadcasted_iota(jnp.int32, (CHUNK, CHUNK), 1)
    causal = s <= t
    for g in range(n_groups):
        w = jnp.where(causal, ws_ref[g], 0.0).astype(BF16)
        bias = bst_ref[:, g:g + 1]
        cols = slice(g * A_GROUP, (g + 1) * A_GROUP)
        for c in range(chunks):
            rows = slice(c * CHUNK, (c + 1) * CHUNK)
            mixed = _dot(w, v_ref[rows, cols]) + bias
            a_ref[rows, cols] = (u_ref[rows, cols] * mixed).astype(a_ref.dtype)


def _gate(u, v, w_s, b_s_t, *, chunks):
    m, w = u.shape
    rows = chunks * CHUNK
    return pl.pallas_call(
        functools.partial(_gate_kernel, chunks=chunks),
        grid=(m // rows,),
        in_specs=[
            pl.BlockSpec((rows, w), lambda i: (i, 0)),
            pl.BlockSpec((rows, w), lambda i: (i, 0)),
            pl.BlockSpec(w_s.shape, lambda i: (0, 0, 0)),
            pl.BlockSpec(b_s_t.shape, lambda i: (0, 0)),
        ],
        out_specs=pl.BlockSpec((rows, w), lambda i: (i, 0)),
        out_shape=jax.ShapeDtypeStruct((m, w), BF16),
        compiler_params=_params("parallel"),
        name="gate",
    )(u, v, w_s, b_s_t)


def _sample_gate_kernel(u_ref, v_ref, w0_ref, b0_ref, a_ref):
    a_ref[...] = (u_ref[...] * (w0_ref[...] * v_ref[...] + b0_ref[...])).astype(a_ref.dtype)


def _sample_gate(u, v, w0, b0):
    m, w = u.shape
    full = lambda shape: pl.BlockSpec(shape, lambda i: (0, 0))
    return pl.pallas_call(
        _sample_gate_kernel,
        grid=(1,),
        in_specs=[full((m, w)), full((m, w)), full((1, w)), full((1, w))],
        out_specs=full((m, w)),
        out_shape=jax.ShapeDtypeStruct((m, w), BF16),
        compiler_params=_params("arbitrary"),
        name="sample_gate",
    )(u, v, w0, b0)


def _sb_prompt_kernel(bias_ref, q_ref, k_ref, v_ref, o_ref, lb_ref, add_ref, acc_ref, *, heads, tq, tk):
    hg = pl.program_id(1)
    qi = pl.program_id(2)
    ratio = tq // tk
    n_blocks = ratio * (qi + 1)
    after = _strict_after_matrix(tk)
    row = lax.broadcasted_iota(jnp.int32, (tq, tk), 0)
    col = lax.broadcasted_iota(jnp.int32, (tq, tk), 1)
    diag_masks = [col < row - (ratio - 1 - s) * tk for s in range(ratio)]

    cols = [slice(hh * HEAD_DIM, (hh + 1) * HEAD_DIM) for hh in range(heads)]
    bias2 = [bias_ref[hg * heads + hh] * LOG2E for hh in range(heads)]
    strips = [slice(r, r + SB_STRIP) for r in range(0, tq, SB_STRIP)]

    def key_rows(s):
        return pl.ds(pl.multiple_of((n_blocks - 1 - s) * tk, tk), tk)

    def logits(s):
        return [_dot_nt(q_ref[:, c], k_ref[key_rows(s), c]) for c in cols]

    def stash_logs(z2, slot, mask):
        sums = []
        for hh, (z, b) in enumerate(zip(z2, bias2)):
            parts = []
            for rows in strips:
                lb, l1m = _stick_logs2(z[rows] + b)
                if mask is not None:
                    lb = jnp.where(mask[rows], lb, MASKED_LOG2)
                    l1m = jnp.where(mask[rows], l1m, 0.0)
                lb_ref[slot, hh, rows, :] = lb
                add_ref[slot, hh, rows, :] = l1m.astype(BF16)
                parts.append(jnp.sum(l1m, axis=1, keepdims=True))
            sums.append(jnp.concatenate(parts, axis=0))
        return sums

    def later_mass(slot):
        return [_dot(add_ref[slot, hh], after) for hh in range(heads)]

    def weights(slot, afts, runs):
        return [jnp.concatenate(
                    [jnp.exp2(lb_ref[slot, hh, rows, :] + aft[rows] + run[rows]).astype(BF16) for rows in strips],
                    axis=0)
                for hh, (aft, run) in enumerate(zip(afts, runs))]

    def accumulate(s, probs):
        for a, c in zip(probs, cols):
            acc_ref[:, c] += _dot(a, v_ref[key_rows(s), c])

    def advance(s, carry, next_mask):
        runs, sums = carry
        slot = s % 2
        afts = later_mass(slot)
        z2 = logits(s + 1)
        accumulate(s, weights(slot, afts, runs))
        next_sums = stash_logs(z2, 1 - slot, next_mask)
        return [r + x for r, x in zip(runs, sums)], next_sums

    acc_ref[...] = jnp.zeros_like(acc_ref)
    carry = ([jnp.zeros((tq, 1), F32) for _ in range(heads)], stash_logs(logits(0), 0, diag_masks[0]))
    for s in range(ratio - 1):
        carry = advance(s, carry, diag_masks[s + 1])
    runs, _ = lax.fori_loop(ratio - 1, n_blocks - 1, lambda s, c: advance(s, c, None), carry)
    last = (n_blocks - 1) % 2
    accumulate(n_blocks - 1, weights(last, later_mass(last), runs))
    o_ref[...] = acc_ref[...].astype(o_ref.dtype)


def _sb_prompt(q, k, v, sb_bias, *, n, t, heads, tq, tk):
    m, w = q.shape
    gw = heads * HEAD_DIM
    nq = t // tq
    assert tq % tk == 0 and t % tq == 0
    return pl.pallas_call(
        functools.partial(_sb_prompt_kernel, heads=heads, tq=tq, tk=tk),
        grid=(n, w // gw, nq),
        in_specs=[
            pl.BlockSpec(memory_space=pltpu.SMEM),
            pl.BlockSpec((tq, gw), lambda b, h, i: (b * nq + i, h)),
            pl.BlockSpec((t, gw), lambda b, h, i: (b, h)),
            pl.BlockSpec((t, gw), lambda b, h, i: (b, h)),
        ],
        out_specs=pl.BlockSpec((tq, gw), lambda b, h, i: (b * nq + i, h)),
        out_shape=jax.ShapeDtypeStruct((m, w), BF16),
        scratch_shapes=[
            pltpu.VMEM((2, heads, tq, tk), F32),
            pltpu.VMEM((2, heads, tq, tk), BF16),
            pltpu.VMEM((tq, gw), F32),
        ],
        compiler_params=_params("parallel", "parallel", "arbitrary"),
        name="sb_prompt",
    )(sb_bias, q, k, v)


def _sum_over_rows(x):
    n = x.shape[0]
    shift = n // 2
    while shift:
        x = x + pltpu.roll(x, shift, 0)
        shift //= 2
    return x


def _sum_of_later_rows(x, row):
    n = x.shape[0]
    up = lambda y, s: jnp.where(row < n - s, pltpu.roll(y, n - s, 0), 0.0)
    y = up(x, 1)
    shift = 1
    while shift < n:
        y = y + up(y, shift)
        shift *= 2
    return y


def _sb_sample_kernel(pt_ref, q_ref, kn_ref, vn_ref, bias_col_ref, bias_lane_ref, *refs, pages, past):
    del pt_ref
    k_refs = refs[:pages]
    v_refs = refs[pages:2 * pages]
    o_ref, run_ref, acc_ref = refs[2 * pages:]
    g = pl.program_id(1)
    n_heads, hd = acc_ref.shape
    n_row, lanes = run_ref.shape
    page_size = k_refs[0].shape[0] // n_heads
    per_row = lanes // n_heads
    q_pos = past
    q = q_ref[0]

    row = lax.broadcasted_iota(jnp.int32, (n_row, lanes), 0)
    lane = lax.broadcasted_iota(jnp.int32, (n_row, lanes), 1)
    own = (lax.broadcasted_iota(jnp.int32, (n_heads, lanes), 1) % n_heads
           == lax.broadcasted_iota(jnp.int32, (n_heads, lanes), 0))

    @pl.when(g == 0)
    def _():
        z = jnp.sum(q.astype(F32) * kn_ref[0], axis=1, keepdims=True) + bias_col_ref[...] * LOG2E
        lb, l1m = _stick_logs2(z)
        seen = jnp.full(z.shape, past, jnp.int32) < q_pos
        l1m = jnp.where(seen, l1m, 0.0)
        run_ref[...] = _sum_over_rows(jnp.where(own, l1m, 0.0))
        acc_ref[...] = jnp.where(seen, jnp.exp2(lb), 0.0) * vn_ref[0]

    src = lax.broadcasted_iota(jnp.int32, (lanes, lanes), 0)
    dst = lax.broadcasted_iota(jnp.int32, (lanes, lanes), 1)
    same_head = src % n_heads == dst % n_heads
    scan = jnp.concatenate([(same_head & (src > dst)).astype(BF16), same_head.astype(BF16)], axis=1)

    bias_row = bias_lane_ref[...] * LOG2E
    first_page = (pl.num_programs(1) - 1 - g) * pages
    run = run_ref[...]
    acc = acc_ref[...]
    scores = [_dot_nt(q, k_ref[...].astype(BF16)) for k_ref in k_refs]
    lbs, seens, sums = [], [], []
    for j in range(pages):
        z = jnp.zeros((n_row, lanes), F32)
        for i in range(n_row):
            mine = jnp.where(own, scores[j][:, i * lanes:(i + 1) * lanes], 0.0)
            z = jnp.where(row == i, _sum_over_rows(mine), z)
        lb, l1m = _stick_logs2(z + bias_row)
        k_pos = (first_page + j) * page_size + row * per_row + lane // n_heads
        seen = k_pos < q_pos
        hi, lo = _split_bf16(jnp.where(seen, l1m, 0.0))
        both = _dot(jnp.concatenate([hi, lo], axis=0), scan)
        lbs.append(lb)
        seens.append(seen)
        sums.append(both[:n_row] + both[n_row:])
    wides = [None] * pages
    for j in reversed(range(pages)):
        in_row, row_total = sums[j][:, :lanes], sums[j][:, lanes:]
        later_rows = _sum_of_later_rows(row_total, row)
        a = jnp.where(seens[j], jnp.exp2(lbs[j] + in_row + later_rows + run), 0.0)
        run = run + jnp.broadcast_to(later_rows[0:1] + row_total[0:1], run.shape)
        wides[j] = jnp.concatenate(
            [jnp.where(own, jnp.broadcast_to(a[i:i + 1], (n_heads, lanes)), 0.0) for i in range(n_row)],
            axis=1).astype(BF16)
    for wide, v_ref in zip(wides, v_refs):
        acc = acc + _dot(wide, v_ref[...].astype(BF16))
    run_ref[...] = run
    acc_ref[...] = acc

    @pl.when(g == pl.num_programs(1) - 1)
    def _():
        o_ref[0] = acc.astype(o_ref.dtype)


def _sb_sample(page_table, q, k_new, v_new, sb_bias, cache_k, cache_v, *, pages):
    nb, n_pages = page_table.shape
    n_phys, page_size, n_heads, hd = cache_k.shape
    page_rows = page_size * n_heads
    assert page_rows % V7X_LANES == 0 and V7X_LANES % n_heads == 0 and n_pages % pages == 0
    n_row = page_rows // V7X_LANES
    assert n_row == n_heads, "the per-head running mass shares the (n_row, lanes) tile shape"
    groups = n_pages // pages
    tok = lambda b, g, pt: (b, 0, 0)
    const = lambda b, g, pt: (0, 0)
    heads = lambda x: x.reshape(nb, n_heads, hd)
    ck = cache_k.reshape(n_phys * page_rows, hd)
    cv = cache_v.reshape(n_phys * page_rows, hd)

    def page_spec(j):
        return pl.BlockSpec((page_rows, hd),
                            lambda b, g, pt: (pt[b * n_pages + (groups - 1 - g) * pages + j], 0))

    grid_spec = pltpu.PrefetchScalarGridSpec(
        num_scalar_prefetch=1,
        grid=(nb, groups),
        in_specs=[
            pl.BlockSpec((1, n_heads, hd), tok),
            pl.BlockSpec((1, n_heads, hd), tok),
            pl.BlockSpec((1, n_heads, hd), tok),
            pl.BlockSpec((n_heads, 1), const),
            pl.BlockSpec((1, V7X_LANES), const),
        ] + [page_spec(j) for j in range(pages)] * 2,
        out_specs=pl.BlockSpec((1, n_heads, hd), tok),
        scratch_shapes=[pltpu.VMEM((n_row, V7X_LANES), F32), pltpu.VMEM((n_heads, hd), F32)],
    )
    out = pl.pallas_call(
        functools.partial(_sb_sample_kernel, pages=pages, past=n_pages * page_size),
        grid_spec=grid_spec,
        out_shape=jax.ShapeDtypeStruct((nb, n_heads, hd), BF16),
        compiler_params=_params("parallel", "arbitrary"),
        name="sb_sample",
    )(page_table.reshape(-1), heads(q), heads(k_new), heads(v_new),
      sb_bias.reshape(n_heads, 1), jnp.tile(sb_bias, V7X_LANES // n_heads).reshape(1, V7X_LANES),
      *([ck] * pages), *([cv] * pages))
    return out.reshape(nb, n_heads * hd)


def _post_mix_kernel(h_ref, a_ref, b_ref, wo_ref, g_ref, bb_ref, o_ref, *, alpha):
    aw = a_ref.shape[1]
    mix = _dot(a_ref[...], wo_ref[:aw, :]) + _dot(b_ref[...], wo_ref[aw:, :])
    o_ref[...] = _layer_norm(alpha * h_ref[...] + mix, g_ref[...], bb_ref[...])


def _post_mix(h, a, b, w_o, g, bb, *, alpha, tm):
    m, d = h.shape
    row = lambda width: pl.BlockSpec((tm, width), lambda i: (i, 0))
    const = lambda shape: pl.BlockSpec(shape, lambda i: (0, 0))
    return pl.pallas_call(
        functools.partial(_post_mix_kernel, alpha=alpha),
        grid=(m // tm,),
        in_specs=[row(d), row(a.shape[1]), row(b.shape[1]), const(w_o.shape), const((1, d)), const((1, d))],
        out_specs=row(d),
        out_shape=jax.ShapeDtypeStruct((m, d), F32),
        compiler_params=_params("parallel"),
        name="post_mix",
    )(h, a, b, w_o, g, bb)


def kernel(x_prompt, x_sample, cache_k, cache_v, page_table, ln1_g, ln1_b, ffn1_w_gate, ffn1_w_up, ffn1_w_down, w_in, gmlp_ln_g, gmlp_ln_b, gmlp_w_s, gmlp_b_s, sb_bias, w_o, ln2_g, ln2_b, ffn2_w_gate, ffn2_w_up, ffn2_w_down, ln3_g, ln3_b):
    depth = w_in.shape[0]
    alpha = (2 * depth) ** 0.25
    n, t, d = x_prompt.shape
    nb, t_new, _ = x_sample.shape
    assert t_new == 1, "the sample path handles one new position per sequence"
    n_heads, hd = cache_k.shape[3], cache_k.shape[4]
    assert hd == HEAD_DIM
    bw = n_heads * hd
    aw = gmlp_ln_g.shape[1]
    n_groups = aw // A_GROUP
    row = lambda p: p.reshape(1, -1)

    xp = x_prompt.reshape(n * t, d)
    xs = x_sample.reshape(nb * t_new, d)
    kp_l, vp_l, ks_l, vs_l, cv_l = [], [], [], [], []
    for l in range(depth):
        bf = lambda p: p[l].astype(BF16)
        wg1, wu1, wd1 = bf(ffn1_w_gate), bf(ffn1_w_up), bf(ffn1_w_down)
        wg2, wu2, wd2 = bf(ffn2_w_gate), bf(ffn2_w_up), bf(ffn2_w_down)
        win, wo = bf(w_in), bf(w_o)
        ffn1 = functools.partial(_ffn_ln, wg=wg1, wu=wu1, wd=wd1, g=row(ln1_g[l]), b=row(ln1_b[l]), alpha=alpha)
        ffn2 = functools.partial(_ffn_ln, wg=wg2, wu=wu2, wd=wd2, g=row(ln3_g[l]), b=row(ln3_b[l]), alpha=alpha)
        proj = functools.partial(_in_proj, w_in=win, lg=row(gmlp_ln_g[l]), lb=row(gmlp_ln_b[l]))
        mix = functools.partial(_post_mix, w_o=wo, g=row(ln2_g[l]), bb=row(ln2_b[l]), alpha=alpha)

        hp, hs = ffn1(xp, xs, tm=1024, tf=512)

        u, v, q, k32, vb32, k16, vb16 = proj(hp, tm=512, v_dtype=BF16)
        a_out = _gate(u, v, gmlp_w_s[l], gmlp_b_s[l].T, chunks=4)
        b_out = _sb_prompt(q, k16, vb16, sb_bias[l], n=n, t=t, heads=4, tq=256, tk=256)
        kp_l.append(k32.reshape(n, t, n_heads, hd))
        vp_l.append(vb32.reshape(n, t, n_heads, hd))

        u_s, v_s, q_s, k_s, vb_s, _, _ = proj(hs, tm=nb, v_dtype=F32)
        w0 = jnp.repeat(gmlp_w_s[l][:, 0, 0], A_GROUP).reshape(1, aw)
        b0 = jnp.repeat(gmlp_b_s[l][:, 0], A_GROUP).reshape(1, aw)
        a_s = _sample_gate(u_s, v_s, w0, b0)
        b_s = _sb_sample(page_table, q_s, k_s, vb_s, sb_bias[l], cache_k[l], cache_v[l], pages=16)
        xp, xs = ffn2(mix(hp, a_out, b_out, tm=512), mix(hs, a_s, b_s, tm=nb), tm=1024, tf=512)
        ks_l.append(k_s.reshape(nb, t_new, n_heads, hd))
        vs_l.append(vb_s.reshape(nb, t_new, n_heads, hd))
        cv_l.append(v_s.reshape(nb, t_new, aw))

    return (xp.reshape(n, t, d), xs.reshape(nb, t_new, d),
            jnp.stack(kp_l), jnp.stack(vp_l), jnp.stack(ks_l), jnp.stack(vs_l), jnp.stack(cv_l))
```

```python
import functools

import jax
import jax.numpy as jnp
from jax import lax
from jax.experimental import pallas as pl
from jax.experimental.pallas import tpu as pltpu

LN_EPS = 1e-5
HEAD_DIM = 128
A_GROUP = 128
CHUNK = 128

V7X_VMEM_LIMIT_BYTES = 60000 * 1024
V7X_LANES = 128

F32 = jnp.float32
BF16 = jnp.bfloat16
LOG2E = 1.4426950408889634
IN_PROJ_CHUNK = 256
MASKED_LOG2 = -1e30


def _params(*semantics):
    return pltpu.CompilerParams(dimension_semantics=semantics,
                                vmem_limit_bytes=V7X_VMEM_LIMIT_BYTES)


def _layer_norm(y, g, b):
    mu = jnp.mean(y, axis=-1, keepdims=True)
    d = y - mu
    var = jnp.mean(d * d, axis=-1, keepdims=True)
    return d * lax.rsqrt(var + LN_EPS) * g + b


def _dot(a, b):
    return jnp.dot(a, b, preferred_element_type=F32)


def _dot_nt(a, b):
    return lax.dot_general(a, b, (((1,), (1,)), ((), ())), preferred_element_type=F32)


def _stick_logs2(z2):
    sp = jnp.log(1.0 + jnp.exp2(-jnp.abs(z2))) * LOG2E
    lb = jnp.minimum(z2, 0.0) - sp
    return lb, lb - z2


def _split_bf16(x):
    hi = x.astype(BF16)
    lo = (x - hi.astype(F32)).astype(BF16)
    return hi, lo


def _strict_after_matrix(n):
    j = lax.broadcasted_iota(jnp.int32, (n, n), 0)
    s = lax.broadcasted_iota(jnp.int32, (n, n), 1)
    return (j > s).astype(BF16)


def _ffn_ln_kernel(x_ref, xs_ref, wg_ref, wu_ref, wd_ref, g_ref, b_ref, o_ref, os_ref, xb_ref, xsb_ref, *, alpha):
    i = pl.program_id(0)
    j = pl.program_id(1)

    def rows(x_ref, o_ref, xb_ref):
        @pl.when(j == 0)
        def _():
            xb_ref[...] = x_ref[...].astype(BF16)
            o_ref[...] = jnp.zeros_like(o_ref)

        xb = xb_ref[...]
        gate = _dot(xb, wg_ref[...].astype(BF16))
        up = _dot(xb, wu_ref[...].astype(BF16))
        h = (jax.nn.silu(gate) * up).astype(BF16)
        o_ref[...] += _dot(h, wd_ref[...].astype(BF16))

        @pl.when(j == pl.num_programs(1) - 1)
        def _():
            y = alpha * x_ref[...] + 0.5 * o_ref[...]
            o_ref[...] = _layer_norm(y, g_ref[...], b_ref[...])

    rows(x_ref, o_ref, xb_ref)

    @pl.when(i == pl.num_programs(0) - 1)
    def _():
        rows(xs_ref, os_ref, xsb_ref)


def _ffn_ln(x, xs, wg, wu, wd, g, b, *, alpha, tm, tf):
    m, d = x.shape
    ms = xs.shape[0]
    d_ff = wg.shape[1]
    once = pl.Buffered(1)
    return pl.pallas_call(
        functools.partial(_ffn_ln_kernel, alpha=alpha),
        grid=(m // tm, d_ff // tf),
        in_specs=[
            pl.BlockSpec((tm, d), lambda i, j: (i, 0), pipeline_mode=once),
            pl.BlockSpec((ms, d), lambda i, j: (0, 0), pipeline_mode=once),
            pl.BlockSpec((d, tf), lambda i, j: (0, j)),
            pl.BlockSpec((d, tf), lambda i, j: (0, j)),
            pl.BlockSpec((tf, d), lambda i, j: (j, 0)),
            pl.BlockSpec((1, d), lambda i, j: (0, 0)),
            pl.BlockSpec((1, d), lambda i, j: (0, 0)),
        ],
        out_specs=[pl.BlockSpec((tm, d), lambda i, j: (i, 0)),
                   pl.BlockSpec((ms, d), lambda i, j: (0, 0))],
        out_shape=[jax.ShapeDtypeStruct((m, d), F32), jax.ShapeDtypeStruct((ms, d), F32)],
        scratch_shapes=[pltpu.VMEM((tm, d), BF16), pltpu.VMEM((ms, d), BF16)],
        compiler_params=_params("arbitrary", "arbitrary"),
        name="ffn_ln",
    )(x, xs, wg, wu, wd, g, b)


def _in_proj_kernel(h_ref, w_ref, lg_ref, lb_ref,
                    u_ref, v_ref, q_ref, k32_ref, vb32_ref, k16_ref, vb16_ref,
                    hb_ref, gelu_ref, *, q_scale):
    c = pl.program_id(1)
    width = w_ref.shape[1]
    chunks = [slice(n0, n0 + IN_PROJ_CHUNK) for n0 in range(0, width, IN_PROJ_CHUNK)]

    @pl.when(c == 0)
    def _():
        hb_ref[...] = h_ref[...].astype(BF16)

    def project(cols):
        return _dot(hb_ref[...], w_ref[:, cols].astype(BF16))

    @pl.when(c == 0)
    def _():
        for cols in chunks:
            u_ref[:, cols] = jax.nn.gelu(project(cols))

    @pl.when(c == 1)
    def _():
        total = jnp.zeros((hb_ref.shape[0], 1), F32)
        for cols in chunks:
            act = jax.nn.gelu(project(cols))
            gelu_ref[:, cols] = act
            total = total + jnp.sum(act, axis=1, keepdims=True)
        d = gelu_ref[...] - total * (1.0 / width)
        var = jnp.mean(d * d, axis=-1, keepdims=True)
        v_ref[...] = (d * lax.rsqrt(var + LN_EPS) * lg_ref[...] + lb_ref[...]).astype(v_ref.dtype)

    @pl.when(c == 2)
    def _():
        for cols in chunks:
            q_ref[:, cols] = (project(cols) * q_scale).astype(BF16)

    @pl.when(c == 3)
    def _():
        for cols in chunks:
            z = project(cols)
            k32_ref[:, cols] = z
            k16_ref[:, cols] = z.astype(BF16)

    @pl.when(c == 4)
    def _():
        for cols in chunks:
            z = project(cols)
            vb32_ref[:, cols] = z
            vb16_ref[:, cols] = z.astype(BF16)


def _in_proj(h, w_in, lg, lb, *, tm, v_dtype):
    m, d = h.shape
    w = w_in.shape[1] // 5
    row = lambda i, c: (i, 0)
    out = lambda dt: jax.ShapeDtypeStruct((m, w), dt)
    return pl.pallas_call(
        functools.partial(_in_proj_kernel, q_scale=HEAD_DIM ** -0.5 * LOG2E),
        grid=(m // tm, 5),
        in_specs=[
            pl.BlockSpec((tm, d), row),
            pl.BlockSpec((d, w), lambda i, c: (0, c)),
            pl.BlockSpec((1, w), lambda i, c: (0, 0)),
            pl.BlockSpec((1, w), lambda i, c: (0, 0)),
        ],
        out_specs=[pl.BlockSpec((tm, w), row)] * 7,
        out_shape=[out(F32), out(v_dtype), out(BF16), out(F32), out(F32), out(BF16), out(BF16)],
        scratch_shapes=[pltpu.VMEM((tm, d), BF16), pltpu.VMEM((tm, w), F32)],
        compiler_params=_params("parallel", "arbitrary"),
        name="in_proj",
    )(h, w_in, lg, lb)


def _gate_kernel(u_ref, v_ref, ws_ref, bst_ref, a_ref, *, chunks):
    n_groups = ws_ref.shape[0]
    t = lax.broadcasted_iota(jnp.int32, (CHUNK, CHUNK), 0)
    s = lax.broadcasted_iota(jnp.int32, (CHUNK, CHUNK), 1)
    causal = s <= t
    for g in range(n_groups):
        w = jnp.where(causal, ws_ref[g], 0.0).astype(BF16)
        bias = bst_ref[:, g:g + 1]
        cols = slice(g * A_GROUP, (g + 1) * A_GROUP)
        for c in range(chunks):
            rows = slice(c * CHUNK, (c + 1) * CHUNK)
            mixed = _dot(w, v_ref[rows, cols]) + bias
            a_ref[rows, cols] = (u_ref[rows, cols] * mixed).astype(a_ref.dtype)


def _gate(u, v, w_s, b_s_t, *, chunks):
    m, w = u.shape
    rows = chunks * CHUNK
    return pl.pallas_call(
        functools.partial(_gate_kernel, chunks=chunks),
        grid=(m // rows,),
        in_specs=[
            pl.BlockSpec((rows, w), lambda i: (i, 0)),
            pl.BlockSpec((rows, w), lambda i: (i, 0)),
            pl.BlockSpec(w_s.shape, lambda i: (0, 0, 0)),
            pl.BlockSpec(b_s_t.shape, lambda i: (0, 0)),
        ],
        out_specs=pl.BlockSpec((rows, w), lambda i: (i, 0)),
        out_shape=jax.ShapeDtypeStruct((m, w), BF16),
        compiler_params=_params("parallel"),
        name="gate",
    )(u, v, w_s, b_s_t)


def _sample_gate_kernel(u_ref, v_ref, w0_ref, b0_ref, a_ref):
    a_ref[...] = (u_ref[...] * (w0_ref[...] * v_ref[...] + b0_ref[...])).astype(a_ref.dtype)


def _sample_gate(u, v, w0, b0):
    m, w = u.shape
    full = lambda shape: pl.BlockSpec(shape, lambda i: (0, 0))
    return pl.pallas_call(
        _sample_gate_kernel,
        grid=(1,),
        in_specs=[full((m, w)), full((m, w)), full((1, w)), full((1, w))],
        out_specs=full((m, w)),
        out_shape=jax.ShapeDtypeStruct((m, w), BF16),
        compiler_params=_params("arbitrary"),
        name="sample_gate",
    )(u, v, w0, b0)


def _sb_prompt_kernel(bias_ref, q_ref, k_ref, v_ref, o_ref, lb_ref, add_ref, acc_ref, *, heads, tq):
    hg = pl.program_id(1)
    qi = pl.program_id(2)
    n_blocks = qi + 1
    after = _strict_after_matrix(tq)
    row = lax.broadcasted_iota(jnp.int32, (tq, tq), 0)
    col = lax.broadcasted_iota(jnp.int32, (tq, tq), 1)
    diag_mask = col < row

    cols = [slice(hh * HEAD_DIM, (hh + 1) * HEAD_DIM) for hh in range(heads)]
    bias2 = [bias_ref[hg * heads + hh] * LOG2E for hh in range(heads)]

    def key_rows(s):
        return pl.ds(pl.multiple_of((n_blocks - 1 - s) * tq, tq), tq)

    def logits(s):
        return [_dot_nt(q_ref[:, c], k_ref[key_rows(s), c]) for c in cols]

    def stash_logs(z2, slot, mask):
        sums = []
        for hh, (z, b) in enumerate(zip(z2, bias2)):
            lb, l1m = _stick_logs2(z + b)
            if mask is not None:
                lb = jnp.where(mask, lb, MASKED_LOG2)
                l1m = jnp.where(mask, l1m, 0.0)
            lb_ref[slot, hh] = lb
            add_ref[slot, hh] = l1m.astype(BF16)
            sums.append(jnp.sum(l1m, axis=1, keepdims=True))
        return sums

    def later_mass(slot):
        return [_dot(add_ref[slot, hh], after) for hh in range(heads)]

    def weights(slot, afts, runs):
        return [jnp.exp2(lb_ref[slot, hh] + aft + run).astype(BF16) for hh, (aft, run) in enumerate(zip(afts, runs))]

    def accumulate(s, probs):
        for a, c in zip(probs, cols):
            acc_ref[:, c] += _dot(a, v_ref[key_rows(s), c])

    def finish(s, slot, runs):
        accumulate(s, weights(slot, later_mass(slot), runs))

    def advance(s, slot, carry):
        runs, sums = carry
        afts = later_mass(slot)
        z_next = logits(s + 1)
        accumulate(s, weights(slot, afts, runs))
        next_sums = stash_logs(z_next, 1 - slot, None)
        return [r + x for r, x in zip(runs, sums)], next_sums

    def pair(p, carry):
        s = 2 * p
        runs, sums = carry
        afts = later_mass(0)
        z_one = logits(s + 1)
        z_two = logits(s + 2)
        accumulate(s, weights(0, afts, runs))
        sums_one = stash_logs(z_one, 1, None)
        runs_one = [r + x for r, x in zip(runs, sums)]
        afts_one = later_mass(1)
        sums_two = stash_logs(z_two, 0, None)
        accumulate(s + 1, weights(1, afts_one, runs_one))
        return [r + x for r, x in zip(runs_one, sums_one)], sums_two

    acc_ref[...] = jnp.zeros_like(acc_ref)
    carry = ([jnp.zeros((tq, 1), F32) for _ in range(heads)], stash_logs(logits(0), 0, diag_mask))
    pairs = (n_blocks - 1) // 2
    runs, sums = lax.fori_loop(0, pairs, pair, carry)
    s_next = 2 * pairs

    @pl.when(n_blocks % 2 == 1)
    def _():
        finish(s_next, 0, runs)

    @pl.when(n_blocks % 2 == 0)
    def _():
        runs_last, _ = advance(s_next, 0, (runs, sums))
        finish(s_next + 1, 1, runs_last)

    o_ref[...] = acc_ref[...].astype(o_ref.dtype)


def _sb_prompt(q, k, v, sb_bias, *, n, t, heads, tq):
    m, w = q.shape
    gw = heads * HEAD_DIM
    nq = t // tq
    return pl.pallas_call(
        functools.partial(_sb_prompt_kernel, heads=heads, tq=tq),
        grid=(n, w // gw, nq),
        in_specs=[
            pl.BlockSpec(memory_space=pltpu.SMEM),
            pl.BlockSpec((tq, gw), lambda b, h, i: (b * nq + i, h)),
            pl.BlockSpec((t, gw), lambda b, h, i: (b, h)),
            pl.BlockSpec((t, gw), lambda b, h, i: (b, h)),
        ],
        out_specs=pl.BlockSpec((tq, gw), lambda b, h, i: (b * nq + i, h)),
        out_shape=jax.ShapeDtypeStruct((m, w), BF16),
        scratch_shapes=[
            pltpu.VMEM((2, heads, tq, tq), F32),
            pltpu.VMEM((2, heads, tq, tq), BF16),
            pltpu.VMEM((tq, gw), F32),
        ],
        compiler_params=_params("parallel", "parallel", "arbitrary"),
        name="sb_prompt",
    )(sb_bias, q, k, v)


def _sum_over_rows(x):
    n = x.shape[0]
    shift = n // 2
    while shift:
        x = x + pltpu.roll(x, shift, 0)
        shift //= 2
    return x


def _sum_of_later_rows(x, row):
    n = x.shape[0]
    up = lambda y, s: jnp.where(row < n - s, pltpu.roll(y, n - s, 0), 0.0)
    y = up(x, 1)
    shift = 1
    while shift < n:
        y = y + up(y, shift)
        shift *= 2
    return y


def _sb_sample_kernel(pt_ref, q_ref, kn_ref, vn_ref, bias_col_ref, bias_lane_ref, *refs, pages, past):
    del pt_ref
    k_refs = refs[:pages]
    v_refs = refs[pages:2 * pages]
    o_ref, run_ref, acc_ref = refs[2 * pages:]
    g = pl.program_id(1)
    n_heads, hd = acc_ref.shape
    n_row, lanes = run_ref.shape
    page_size = k_refs[0].shape[0] // n_heads
    per_row = lanes // n_heads
    q_pos = past
    q = q_ref[0]

    row = lax.broadcasted_iota(jnp.int32, (n_row, lanes), 0)
    lane = lax.broadcasted_iota(jnp.int32, (n_row, lanes), 1)
    own = (lax.broadcasted_iota(jnp.int32, (n_heads, lanes), 1) % n_heads
           == lax.broadcasted_iota(jnp.int32, (n_heads, lanes), 0))

    @pl.when(g == 0)
    def _():
        z = jnp.sum(q.astype(F32) * kn_ref[0], axis=1, keepdims=True) + bias_col_ref[...] * LOG2E
        lb, l1m = _stick_logs2(z)
        seen = jnp.full(z.shape, past, jnp.int32) < q_pos
        l1m = jnp.where(seen, l1m, 0.0)
        run_ref[...] = _sum_over_rows(jnp.where(own, l1m, 0.0))
        acc_ref[...] = jnp.where(seen, jnp.exp2(lb), 0.0) * vn_ref[0]

    src = lax.broadcasted_iota(jnp.int32, (lanes, lanes), 0)
    dst = lax.broadcasted_iota(jnp.int32, (lanes, lanes), 1)
    same_head = src % n_heads == dst % n_heads
    scan = jnp.concatenate([(same_head & (src > dst)).astype(BF16), same_head.astype(BF16)], axis=1)

    bias_row = bias_lane_ref[...] * LOG2E
    first_page = (pl.num_programs(1) - 1 - g) * pages
    run = run_ref[...]
    acc = acc_ref[...]
    scores = [_dot_nt(q, k_ref[...].astype(BF16)) for k_ref in k_refs]
    lbs, seens, sums = [], [], []
    for j in range(pages):
        z = jnp.zeros((n_row, lanes), F32)
        for i in range(n_row):
            mine = jnp.where(own, scores[j][:, i * lanes:(i + 1) * lanes], 0.0)
            z = jnp.where(row == i, _sum_over_rows(mine), z)
        lb, l1m = _stick_logs2(z + bias_row)
        k_pos = (first_page + j) * page_size + row * per_row + lane // n_heads
        seen = k_pos < q_pos
        hi, lo = _split_bf16(jnp.where(seen, l1m, 0.0))
        both = _dot(jnp.concatenate([hi, lo], axis=0), scan)
        lbs.append(lb)
        seens.append(seen)
        sums.append(both[:n_row] + both[n_row:])
    wides = [None] * pages
    for j in reversed(range(pages)):
        in_row, row_total = sums[j][:, :lanes], sums[j][:, lanes:]
        later_rows = _sum_of_later_rows(row_total, row)
        a = jnp.where(seens[j], jnp.exp2(lbs[j] + in_row + later_rows + run), 0.0)
        run = run + jnp.broadcast_to(later_rows[0:1] + row_total[0:1], run.shape)
        wides[j] = jnp.concatenate(
            [jnp.where(own, jnp.broadcast_to(a[i:i + 1], (n_heads, lanes)), 0.0) for i in range(n_row)],
            axis=1).astype(BF16)
    for wide, v_ref in zip(wides, v_refs):
        acc = acc + _dot(wide, v_ref[...].astype(BF16))
    run_ref[...] = run
    acc_ref[...] = acc

    @pl.when(g == pl.num_programs(1) - 1)
    def _():
        o_ref[0] = acc.astype(o_ref.dtype)


def _sb_sample(page_table, q, k_new, v_new, sb_bias, cache_k, cache_v, *, pages):
    nb, n_pages = page_table.shape
    n_phys, page_size, n_heads, hd = cache_k.shape
    page_rows = page_size * n_heads
    assert page_rows % V7X_LANES == 0 and V7X_LANES % n_heads == 0 and n_pages % pages == 0
    n_row = page_rows // V7X_LANES
    assert n_row == n_heads, "the per-head running mass shares the (n_row, lanes) tile shape"
    groups = n_pages // pages
    tok = lambda b, g, pt: (b, 0, 0)
    const = lambda b, g, pt: (0, 0)
    heads = lambda x: x.reshape(nb, n_heads, hd)
    ck = cache_k.reshape(n_phys * page_rows, hd)
    cv = cache_v.reshape(n_phys * page_rows, hd)

    def page_spec(j):
        return pl.BlockSpec((page_rows, hd),
                            lambda b, g, pt: (pt[b * n_pages + (groups - 1 - g) * pages + j], 0))

    grid_spec = pltpu.PrefetchScalarGridSpec(
        num_scalar_prefetch=1,
        grid=(nb, groups),
        in_specs=[
            pl.BlockSpec((1, n_heads, hd), tok),
            pl.BlockSpec((1, n_heads, hd), tok),
            pl.BlockSpec((1, n_heads, hd), tok),
            pl.BlockSpec((n_heads, 1), const),
            pl.BlockSpec((1, V7X_LANES), const),
        ] + [page_spec(j) for j in range(pages)] * 2,
        out_specs=pl.BlockSpec((1, n_heads, hd), tok),
        scratch_shapes=[pltpu.VMEM((n_row, V7X_LANES), F32), pltpu.VMEM((n_heads, hd), F32)],
    )
    out = pl.pallas_call(
        functools.partial(_sb_sample_kernel, pages=pages, past=n_pages * page_size),
        grid_spec=grid_spec,
        out_shape=jax.ShapeDtypeStruct((nb, n_heads, hd), BF16),
        compiler_params=_params("parallel", "arbitrary"),
        name="sb_sample",
    )(page_table.reshape(-1), heads(q), heads(k_new), heads(v_new),
      sb_bias.reshape(n_heads, 1), jnp.tile(sb_bias, V7X_LANES // n_heads).reshape(1, V7X_LANES),
      *([ck] * pages), *([cv] * pages))
    return out.reshape(nb, n_heads * hd)


def _post_mix_kernel(h_ref, a_ref, b_ref, wo_ref, g_ref, bb_ref, o_ref, *, alpha):
    aw = a_ref.shape[1]
    mix = _dot(a_ref[...], wo_ref[:aw, :]) + _dot(b_ref[...], wo_ref[aw:, :])
    o_ref[...] = _layer_norm(alpha * h_ref[...] + mix, g_ref[...], bb_ref[...])


def _post_mix(h, a, b, w_o, g, bb, *, alpha, tm):
    m, d = h.shape
    row = lambda width: pl.BlockSpec((tm, width), lambda i: (i, 0))
    const = lambda shape: pl.BlockSpec(shape, lambda i: (0, 0))
    return pl.pallas_call(
        functools.partial(_post_mix_kernel, alpha=alpha),
        grid=(m // tm,),
        in_specs=[row(d), row(a.shape[1]), row(b.shape[1]), const(w_o.shape), const((1, d)), const((1, d))],
        out_specs=row(d),
        out_shape=jax.ShapeDtypeStruct((m, d), F32),
        compiler_params=_params("parallel"),
        name="post_mix",
    )(h, a, b, w_o, g, bb)


def kernel(x_prompt, x_sample, cache_k, cache_v, page_table, ln1_g, ln1_b, ffn1_w_gate, ffn1_w_up, ffn1_w_down, w_in, gmlp_ln_g, gmlp_ln_b, gmlp_w_s, gmlp_b_s, sb_bias, w_o, ln2_g, ln2_b, ffn2_w_gate, ffn2_w_up, ffn2_w_down, ln3_g, ln3_b):
    depth = w_in.shape[0]
    alpha = (2 * depth) ** 0.25
    n, t, d = x_prompt.shape
    nb, t_new, _ = x_sample.shape
    assert t_new == 1, "the sample path handles one new position per sequence"
    n_heads, hd = cache_k.shape[3], cache_k.shape[4]
    assert hd == HEAD_DIM
    bw = n_heads * hd
    aw = gmlp_ln_g.shape[1]
    n_groups = aw // A_GROUP
    row = lambda p: p.reshape(1, -1)

    xp = x_prompt.reshape(n * t, d)
    xs = x_sample.reshape(nb * t_new, d)
    kp_l, vp_l, ks_l, vs_l, cv_l = [], [], [], [], []
    for l in range(depth):
        bf = lambda p: p[l].astype(BF16)
        wg1, wu1, wd1 = ffn1_w_gate[l], ffn1_w_up[l], ffn1_w_down[l]
        wg2, wu2, wd2 = ffn2_w_gate[l], ffn2_w_up[l], ffn2_w_down[l]
        win, wo = bf(w_in), bf(w_o)
        ffn1 = functools.partial(_ffn_ln, wg=wg1, wu=wu1, wd=wd1, g=row(ln1_g[l]), b=row(ln1_b[l]), alpha=alpha)
        ffn2 = functools.partial(_ffn_ln, wg=wg2, wu=wu2, wd=wd2, g=row(ln3_g[l]), b=row(ln3_b[l]), alpha=alpha)
        proj = functools.partial(_in_proj, w_in=win, lg=row(gmlp_ln_g[l]), lb=row(gmlp_ln_b[l]))
        mix = functools.partial(_post_mix, w_o=wo, g=row(ln2_g[l]), bb=row(ln2_b[l]), alpha=alpha)

        hp, hs = ffn1(xp, xs, tm=1024, tf=256)

        u, v, q, k32, vb32, k16, vb16 = proj(hp, tm=512, v_dtype=BF16)
        a_out = _gate(u, v, gmlp_w_s[l], gmlp_b_s[l].T, chunks=4)
        b_out = _sb_prompt(q, k16, vb16, sb_bias[l], n=n, t=t, heads=4, tq=256)
        kp_l.append(k32.reshape(n, t, n_heads, hd))
        vp_l.append(vb32.reshape(n, t, n_heads, hd))

        u_s, v_s, q_s, k_s, vb_s, _, _ = proj(hs, tm=nb, v_dtype=F32)
        w0 = jnp.repeat(gmlp_w_s[l][:, 0, 0], A_GROUP).reshape(1, aw)
        b0 = jnp.repeat(gmlp_b_s[l][:, 0], A_GROUP).reshape(1, aw)
        a_s = _sample_gate(u_s, v_s, w0, b0)
        b_s = _sb_sample(page_table, q_s, k_s, vb_s, sb_bias[l], cache_k[l], cache_v[l], pages=16)
        xp, xs = ffn2(mix(hp, a_out, b_out, tm=512), mix(hs, a_s, b_s, tm=nb), tm=1024, tf=256)
        ks_l.append(k_s.reshape(nb, t_new, n_heads, hd))
        vs_l.append(vb_s.reshape(nb, t_new, n_heads, hd))
        cv_l.append(v_s.reshape(nb, t_new, aw))

    return (xp.reshape(n, t, d), xs.reshape(nb, t_new, d),
            jnp.stack(kp_l), jnp.stack(vp_l), jnp.stack(ks_l), jnp.stack(vs_l), jnp.stack(cv_l))
```

```python
import functools
from typing import NamedTuple

import jax
import jax.numpy as jnp
from jax import lax
from jax.experimental import pallas as pl
from jax.experimental.pallas import tpu as pltpu

LN_EPS = 1e-5
HEAD_DIM = 128
A_GROUP = 128
CHUNK = 128

V7X_VMEM_LIMIT_BYTES = 62 * 1024 * 1024
V7X_LANES = 128
V7X_MXU_DIM = 256

F32 = jnp.float32
BF16 = jnp.bfloat16
LOG2E = 1.4426950408889634
IN_PROJ_CHUNK = V7X_MXU_DIM
MASKED_LOG2 = -1e30


class _Tiles(NamedTuple):
    ffn_rows: int
    ffn_cols: int
    proj_rows: int
    sb_rows: int
    sb_heads: int
    sb_pages: int


def _plan_tiles(d_model, in_cols, n_pages, page_bytes):
    tiles = _Tiles(ffn_rows=1024, ffn_cols=V7X_MXU_DIM, proj_rows=512, sb_rows=V7X_MXU_DIM, sb_heads=4,
                   sb_pages=n_pages)
    f32, bf16 = 4, 2
    out_bytes = 3 * f32 + 4 * bf16
    need = {
        "ffn_ln": tiles.ffn_rows * d_model * (f32 + bf16 + 2 * f32) + 2 * 3 * d_model * tiles.ffn_cols * f32,
        "in_proj": (d_model * in_cols * bf16 + tiles.proj_rows * d_model * (2 * f32 + bf16)
                    + tiles.proj_rows * (in_cols // 5) * (f32 + 2 * out_bytes)),
        "sb_sample": 2 * 2 * tiles.sb_pages * page_bytes,
    }
    for name, nbytes in need.items():
        assert nbytes <= V7X_VMEM_LIMIT_BYTES, (name, nbytes)
    return tiles


def _params(*semantics):
    return pltpu.CompilerParams(dimension_semantics=semantics,
                                vmem_limit_bytes=V7X_VMEM_LIMIT_BYTES)


def _layer_norm(y, g, b):
    mu = jnp.mean(y, axis=-1, keepdims=True)
    d = y - mu
    var = jnp.mean(d * d, axis=-1, keepdims=True)
    return d * lax.rsqrt(var + LN_EPS) * g + b


def _dot(a, b):
    return jnp.dot(a, b, preferred_element_type=F32)


def _dot_nt(a, b):
    return lax.dot_general(a, b, (((1,), (1,)), ((), ())), preferred_element_type=F32)


def _stick_logs2(z2):
    sp = jnp.log(1.0 + jnp.exp2(-jnp.abs(z2))) * LOG2E
    lb = jnp.minimum(z2, 0.0) - sp
    return lb, lb - z2


def _split_bf16(x):
    hi = x.astype(BF16)
    lo = (x - hi.astype(F32)).astype(BF16)
    return hi, lo


def _strict_after_matrix(n):
    j = lax.broadcasted_iota(jnp.int32, (n, n), 0)
    s = lax.broadcasted_iota(jnp.int32, (n, n), 1)
    return (j > s).astype(BF16)


def _ffn_ln_kernel(x_ref, xs_ref, wg_ref, wu_ref, wd_ref, g_ref, b_ref, o_ref, os_ref, xb_ref, *, alpha):
    i = pl.program_id(0)
    j = pl.program_id(1)
    tm = x_ref.shape[0]
    with_sample = i == pl.num_programs(0) - 1
    first_cols = j == 0
    last_cols = j == pl.num_programs(1) - 1

    @pl.when(first_cols)
    def _():
        xb_ref[:tm, :] = x_ref[...].astype(BF16)
        o_ref[...] = jnp.zeros_like(o_ref)

    @pl.when(first_cols & with_sample)
    def _():
        xb_ref[tm:, :] = xs_ref[...].astype(BF16)
        os_ref[...] = jnp.zeros_like(os_ref)

    def swiglu(xb):
        gate = _dot(xb, wg_ref[...].astype(BF16))
        up = _dot(xb, wu_ref[...].astype(BF16))
        h = (jax.nn.silu(gate) * up).astype(BF16)
        return _dot(h, wd_ref[...].astype(BF16))

    @pl.when(jnp.logical_not(with_sample))
    def _():
        o_ref[...] += swiglu(xb_ref[:tm, :])

    @pl.when(with_sample)
    def _():
        y = swiglu(xb_ref[...])
        o_ref[...] += y[:tm]
        os_ref[...] += y[tm:]

    def finish(x_ref, o_ref):
        y = alpha * x_ref[...] + 0.5 * o_ref[...]
        o_ref[...] = _layer_norm(y, g_ref[...], b_ref[...])

    @pl.when(last_cols)
    def _():
        finish(x_ref, o_ref)

    @pl.when(last_cols & with_sample)
    def _():
        finish(xs_ref, os_ref)


def _ffn_ln(x, xs, wg, wu, wd, g, b, *, alpha, tm, tf):
    m, d = x.shape
    ms = xs.shape[0]
    d_ff = wg.shape[1]
    once = pl.Buffered(1)
    return pl.pallas_call(
        functools.partial(_ffn_ln_kernel, alpha=alpha),
        grid=(m // tm, d_ff // tf),
        in_specs=[
            pl.BlockSpec((tm, d), lambda i, j: (i, 0)),
            pl.BlockSpec((ms, d), lambda i, j: (0, 0), pipeline_mode=once),
            pl.BlockSpec((d, tf), lambda i, j: (0, j)),
            pl.BlockSpec((d, tf), lambda i, j: (0, j)),
            pl.BlockSpec((tf, d), lambda i, j: (j, 0)),
            pl.BlockSpec((1, d), lambda i, j: (0, 0)),
            pl.BlockSpec((1, d), lambda i, j: (0, 0)),
        ],
        out_specs=[pl.BlockSpec((tm, d), lambda i, j: (i, 0)),
                   pl.BlockSpec((ms, d), lambda i, j: (0, 0))],
        out_shape=[jax.ShapeDtypeStruct((m, d), F32), jax.ShapeDtypeStruct((ms, d), F32)],
        scratch_shapes=[pltpu.VMEM((tm + ms, d), BF16)],
        compiler_params=_params("arbitrary", "arbitrary"),
        name="ffn_ln",
    )(x, xs, wg, wu, wd, g, b)


def _in_proj_kernel(h_ref, w_ref, lg_ref, lb_ref,
                    u_ref, v_ref, q_ref, k32_ref, vb32_ref, k16_ref, vb16_ref,
                    hb_ref, gelu_ref, *, q_scale):
    width = u_ref.shape[1]
    chunks = [slice(n0, n0 + IN_PROJ_CHUNK) for n0 in range(0, width, IN_PROJ_CHUNK)]
    hb_ref[...] = h_ref[...].astype(BF16)

    def project(group, cols):
        start = group * width + cols.start
        return _dot(hb_ref[...], w_ref[:, start:start + IN_PROJ_CHUNK])

    for cols in chunks:
        u_ref[:, cols] = jax.nn.gelu(project(0, cols))

    total = jnp.zeros((hb_ref.shape[0], 1), F32)
    for cols in chunks:
        act = jax.nn.gelu(project(1, cols))
        gelu_ref[:, cols] = act
        total = total + jnp.sum(act, axis=1, keepdims=True)
    d = gelu_ref[...] - total * (1.0 / width)
    var = jnp.mean(d * d, axis=-1, keepdims=True)
    v_ref[...] = (d * lax.rsqrt(var + LN_EPS) * lg_ref[...] + lb_ref[...]).astype(v_ref.dtype)

    for cols in chunks:
        q_ref[:, cols] = (project(2, cols) * q_scale).astype(BF16)
    for group, (f32_ref, bf16_ref) in ((3, (k32_ref, k16_ref)), (4, (vb32_ref, vb16_ref))):
        for cols in chunks:
            z = project(group, cols)
            f32_ref[:, cols] = z
            bf16_ref[:, cols] = z.astype(BF16)


def _in_proj(h, w_in, lg, lb, *, tm, v_dtype):
    m, d = h.shape
    w = w_in.shape[1] // 5
    row = lambda i: (i, 0)
    const = lambda i: (0, 0)
    out = lambda dt: jax.ShapeDtypeStruct((m, w), dt)
    return pl.pallas_call(
        functools.partial(_in_proj_kernel, q_scale=HEAD_DIM ** -0.5 * LOG2E),
        grid=(m // tm,),
        in_specs=[
            pl.BlockSpec((tm, d), row),
            pl.BlockSpec(w_in.shape, const, pipeline_mode=pl.Buffered(1)),
            pl.BlockSpec((1, w), const),
            pl.BlockSpec((1, w), const),
        ],
        out_specs=[pl.BlockSpec((tm, w), row)] * 7,
        out_shape=[out(F32), out(v_dtype), out(BF16), out(F32), out(F32), out(BF16), out(BF16)],
        scratch_shapes=[pltpu.VMEM((tm, d), BF16), pltpu.VMEM((tm, w), F32)],
        compiler_params=_params("parallel"),
        name="in_proj",
    )(h, w_in, lg, lb)


def _gate_rows(u_ref, v_ref, ws_ref, bst_ref, a_ref):
    n_groups = ws_ref.shape[0]
    chunks = u_ref.shape[0] // CHUNK
    t = lax.broadcasted_iota(jnp.int32, (CHUNK, CHUNK), 0)
    s = lax.broadcasted_iota(jnp.int32, (CHUNK, CHUNK), 1)
    causal = s <= t
    for g in range(n_groups):
        w = jnp.where(causal, ws_ref[g], 0.0).astype(BF16)
        bias = bst_ref[:, g:g + 1]
        cols = slice(g * A_GROUP, (g + 1) * A_GROUP)
        for c in range(chunks):
            rows = slice(c * CHUNK, (c + 1) * CHUNK)
            mixed = _dot(w, v_ref[rows, cols]) + bias
            a_ref[rows, cols] = (u_ref[rows, cols] * mixed).astype(a_ref.dtype)


def _sample_gate_kernel(u_ref, v_ref, w0_ref, b0_ref, a_ref):
    a_ref[...] = (u_ref[...] * (w0_ref[...] * v_ref[...] + b0_ref[...])).astype(a_ref.dtype)


def _sample_gate(u, v, w0, b0):
    m, w = u.shape
    full = lambda shape: pl.BlockSpec(shape, lambda i: (0, 0))
    return pl.pallas_call(
        _sample_gate_kernel,
        grid=(1,),
        in_specs=[full((m, w)), full((m, w)), full((1, w)), full((1, w))],
        out_specs=full((m, w)),
        out_shape=jax.ShapeDtypeStruct((m, w), BF16),
        compiler_params=_params("arbitrary"),
        name="sample_gate",
    )(u, v, w0, b0)


def _sb_prompt_kernel(bias_ref, q_ref, k_ref, v_ref, o_ref, lb_ref, add_ref, acc_ref, *, heads, tq):
    hg = pl.program_id(1)
    qi = pl.program_id(2)
    n_blocks = qi + 1
    after = _strict_after_matrix(tq)
    row = lax.broadcasted_iota(jnp.int32, (tq, tq), 0)
    col = lax.broadcasted_iota(jnp.int32, (tq, tq), 1)
    diag_mask = col < row

    cols = [slice(hh * HEAD_DIM, (hh + 1) * HEAD_DIM) for hh in range(heads)]
    bias2 = [bias_ref[hg * heads + hh] * LOG2E for hh in range(heads)]

    def key_rows(s):
        return pl.ds(pl.multiple_of((n_blocks - 1 - s) * tq, tq), tq)

    def logits(s):
        return [_dot_nt(q_ref[:, c], k_ref[key_rows(s), c]) for c in cols]

    def stash_logs(z2, slot, mask):
        sums = []
        for hh, (z, b) in enumerate(zip(z2, bias2)):
            lb, l1m = _stick_logs2(z + b)
            if mask is not None:
                lb = jnp.where(mask, lb, MASKED_LOG2)
                l1m = jnp.where(mask, l1m, 0.0)
            lb_ref[slot, hh] = lb
            add_ref[slot, hh] = l1m.astype(BF16)
            sums.append(jnp.sum(l1m, axis=1, keepdims=True))
        return sums

    def later_mass(slot):
        return [_dot(add_ref[slot, hh], after) for hh in range(heads)]

    def weights(slot, afts, runs):
        return [jnp.exp2(lb_ref[slot, hh] + aft + run).astype(BF16) for hh, (aft, run) in enumerate(zip(afts, runs))]

    def accumulate(s, probs):
        for a, c in zip(probs, cols):
            acc_ref[:, c] += _dot(a, v_ref[key_rows(s), c])

    def finish(s, slot, runs):
        accumulate(s, weights(slot, later_mass(slot), runs))

    def advance(s, slot, carry):
        runs, sums = carry
        afts = later_mass(slot)
        z_next = logits(s + 1)
        accumulate(s, weights(slot, afts, runs))
        next_sums = stash_logs(z_next, 1 - slot, None)
        return [r + x for r, x in zip(runs, sums)], next_sums

    def pair(p, carry):
        s = 2 * p
        runs, sums = carry
        afts = later_mass(0)
        z_one = logits(s + 1)
        z_two = logits(s + 2)
        accumulate(s, weights(0, afts, runs))
        sums_one = stash_logs(z_one, 1, None)
        runs_one = [r + x for r, x in zip(runs, sums)]
        afts_one = later_mass(1)
        sums_two = stash_logs(z_two, 0, None)
        accumulate(s + 1, weights(1, afts_one, runs_one))
        return [r + x for r, x in zip(runs_one, sums_one)], sums_two

    acc_ref[...] = jnp.zeros_like(acc_ref)
    carry = ([jnp.zeros((tq, 1), F32) for _ in range(heads)], stash_logs(logits(0), 0, diag_mask))
    pairs = (n_blocks - 1) // 2
    runs, sums = lax.fori_loop(0, pairs, pair, carry)
    s_next = 2 * pairs

    @pl.when(n_blocks % 2 == 1)
    def _():
        finish(s_next, 0, runs)

    @pl.when(n_blocks % 2 == 0)
    def _():
        runs_last, _ = advance(s_next, 0, (runs, sums))
        finish(s_next + 1, 1, runs_last)

    o_ref[...] = acc_ref[...].astype(o_ref.dtype)


def _sb_prompt(q, k, v, sb_bias, *, n, t, heads, tq):
    m, w = q.shape
    gw = heads * HEAD_DIM
    nq = t // tq
    return pl.pallas_call(
        functools.partial(_sb_prompt_kernel, heads=heads, tq=tq),
        grid=(n, w // gw, nq),
        in_specs=[
            pl.BlockSpec(memory_space=pltpu.SMEM),
            pl.BlockSpec((tq, gw), lambda b, h, i: (b * nq + i, h)),
            pl.BlockSpec((t, gw), lambda b, h, i: (b, h)),
            pl.BlockSpec((t, gw), lambda b, h, i: (b, h)),
        ],
        out_specs=pl.BlockSpec((tq, gw), lambda b, h, i: (b * nq + i, h)),
        out_shape=jax.ShapeDtypeStruct((m, w), BF16),
        scratch_shapes=[
            pltpu.VMEM((2, heads, tq, tq), F32),
            pltpu.VMEM((2, heads, tq, tq), BF16),
            pltpu.VMEM((tq, gw), F32),
        ],
        compiler_params=_params("parallel", "parallel", "arbitrary"),
        name="sb_prompt",
    )(sb_bias, q, k, v)


def _sum_over_rows(x):
    n = x.shape[0]
    shift = n // 2
    while shift:
        x = x + pltpu.roll(x, shift, 0)
        shift //= 2
    return x


def _sum_of_later_rows(x, row):
    n = x.shape[0]
    up = lambda y, s: jnp.where(row < n - s, pltpu.roll(y, n - s, 0), 0.0)
    y = up(x, 1)
    shift = 1
    while shift < n:
        y = y + up(y, shift)
        shift *= 2
    return y


def _sb_sample_kernel(pt_ref, q_ref, kn_ref, vn_ref, bias_col_ref, bias_lane_ref, *refs, pages, past):
    del pt_ref
    k_refs = refs[:pages]
    v_refs = refs[pages:2 * pages]
    o_ref, run_ref, acc_ref = refs[2 * pages:]
    g = pl.program_id(1)
    n_heads, hd = acc_ref.shape
    n_row, lanes = run_ref.shape
    page_size = k_refs[0].shape[0] // n_heads
    per_row = lanes // n_heads
    q_pos = past
    q = q_ref[0]

    row = lax.broadcasted_iota(jnp.int32, (n_row, lanes), 0)
    lane = lax.broadcasted_iota(jnp.int32, (n_row, lanes), 1)
    own = (lax.broadcasted_iota(jnp.int32, (n_heads, lanes), 1) % n_heads
           == lax.broadcasted_iota(jnp.int32, (n_heads, lanes), 0))

    @pl.when(g == 0)
    def _():
        z = jnp.sum(q.astype(F32) * kn_ref[0], axis=1, keepdims=True) + bias_col_ref[...] * LOG2E
        lb, l1m = _stick_logs2(z)
        seen = jnp.full(z.shape, past, jnp.int32) < q_pos
        l1m = jnp.where(seen, l1m, 0.0)
        run_ref[...] = _sum_over_rows(jnp.where(own, l1m, 0.0))
        acc_ref[...] = jnp.where(seen, jnp.exp2(lb), 0.0) * vn_ref[0]

    src = lax.broadcasted_iota(jnp.int32, (lanes, lanes), 0)
    dst = lax.broadcasted_iota(jnp.int32, (lanes, lanes), 1)
    same_head = src % n_heads == dst % n_heads
    scan = jnp.concatenate([(same_head & (src > dst)).astype(BF16), same_head.astype(BF16)], axis=1)

    bias_row = bias_lane_ref[...] * LOG2E
    first_page = (pl.num_programs(1) - 1 - g) * pages
    run = run_ref[...]
    acc = acc_ref[...]
    scores = [_dot_nt(q, k_ref[...].astype(BF16)) for k_ref in k_refs]
    lbs, seens, sums = [], [], []
    for j in range(pages):
        z = jnp.zeros((n_row, lanes), F32)
        for i in range(n_row):
            mine = jnp.where(own, scores[j][:, i * lanes:(i + 1) * lanes], 0.0)
            z = jnp.where(row == i, _sum_over_rows(mine), z)
        lb, l1m = _stick_logs2(z + bias_row)
        k_pos = (first_page + j) * page_size + row * per_row + lane // n_heads
        seen = k_pos < q_pos
        hi, lo = _split_bf16(jnp.where(seen, l1m, 0.0))
        both = _dot(jnp.concatenate([hi, lo], axis=0), scan)
        lbs.append(lb)
        seens.append(seen)
        sums.append(both[:n_row] + both[n_row:])
    wides = [None] * pages
    for j in reversed(range(pages)):
        in_row, row_total = sums[j][:, :lanes], sums[j][:, lanes:]
        later_rows = _sum_of_later_rows(row_total, row)
        a = jnp.where(seens[j], jnp.exp2(lbs[j] + in_row + later_rows + run), 0.0)
        run = run + jnp.broadcast_to(later_rows[0:1] + row_total[0:1], run.shape)
        wides[j] = jnp.concatenate(
            [jnp.where(own, jnp.broadcast_to(a[i:i + 1], (n_heads, lanes)), 0.0) for i in range(n_row)],
            axis=1).astype(BF16)
    for wide, v_ref in zip(wides, v_refs):
        acc = acc + _dot(wide, v_ref[...].astype(BF16))
    run_ref[...] = run
    acc_ref[...] = acc

    @pl.when(g == pl.num_programs(1) - 1)
    def _():
        o_ref[0] = acc.astype(o_ref.dtype)


def _sb_sample(page_table, q, k_new, v_new, sb_bias, cache_k, cache_v, *, pages):
    nb, n_pages = page_table.shape
    n_phys, page_size, n_heads, hd = cache_k.shape
    page_rows = page_size * n_heads
    assert page_rows % V7X_LANES == 0 and V7X_LANES % n_heads == 0 and n_pages % pages == 0
    n_row = page_rows // V7X_LANES
    assert n_row == n_heads, "the per-head running mass shares the (n_row, lanes) tile shape"
    groups = n_pages // pages
    tok = lambda b, g, pt: (b, 0, 0)
    const = lambda b, g, pt: (0, 0)
    heads = lambda x: x.reshape(nb, n_heads, hd)
    ck = cache_k.reshape(n_phys * page_rows, hd)
    cv = cache_v.reshape(n_phys * page_rows, hd)

    def page_spec(j):
        return pl.BlockSpec((page_rows, hd),
                            lambda b, g, pt: (pt[b * n_pages + (groups - 1 - g) * pages + j], 0))

    grid_spec = pltpu.PrefetchScalarGridSpec(
        num_scalar_prefetch=1,
        grid=(nb, groups),
        in_specs=[
            pl.BlockSpec((1, n_heads, hd), tok),
            pl.BlockSpec((1, n_heads, hd), tok),
            pl.BlockSpec((1, n_heads, hd), tok),
            pl.BlockSpec((n_heads, 1), const),
            pl.BlockSpec((1, V7X_LANES), const),
        ] + [page_spec(j) for j in range(pages)] * 2,
        out_specs=pl.BlockSpec((1, n_heads, hd), tok),
        scratch_shapes=[pltpu.VMEM((n_row, V7X_LANES), F32), pltpu.VMEM((n_heads, hd), F32)],
    )
    out = pl.pallas_call(
        functools.partial(_sb_sample_kernel, pages=pages, past=n_pages * page_size),
        grid_spec=grid_spec,
        out_shape=jax.ShapeDtypeStruct((nb, n_heads, hd), BF16),
        compiler_params=_params("parallel", "arbitrary"),
        name="sb_sample",
    )(page_table.reshape(-1), heads(q), heads(k_new), heads(v_new),
      sb_bias.reshape(n_heads, 1), jnp.tile(sb_bias, V7X_LANES // n_heads).reshape(1, V7X_LANES),
      *([ck] * pages), *([cv] * pages))
    return out.reshape(nb, n_heads * hd)


def _post_mix_kernel(h_ref, a_ref, b_ref, wo_ref, g_ref, bb_ref, o_ref, *, alpha):
    aw = a_ref.shape[1]
    mix = _dot(b_ref[...], wo_ref[aw:, :]) + _dot(a_ref[...], wo_ref[:aw, :])
    o_ref[...] = _layer_norm(alpha * h_ref[...] + mix, g_ref[...], bb_ref[...])


def _gated_post_mix_kernel(h_ref, u_ref, v_ref, ws_ref, bst_ref, b_ref, wo_ref, g_ref, bb_ref, o_ref, a_ref, *,
                           alpha):
    aw = a_ref.shape[1]
    mix_b = _dot(b_ref[...], wo_ref[aw:, :])
    _gate_rows(u_ref, v_ref, ws_ref, bst_ref, a_ref)
    mix = mix_b + _dot(a_ref[...], wo_ref[:aw, :])
    o_ref[...] = _layer_norm(alpha * h_ref[...] + mix, g_ref[...], bb_ref[...])


def _post_mix(h, a, b, w_o, g, bb, *, alpha, tm):
    m, d = h.shape
    row = lambda width: pl.BlockSpec((tm, width), lambda i: (i, 0))
    const = lambda shape: pl.BlockSpec(shape, lambda i: (0, 0))
    return pl.pallas_call(
        functools.partial(_post_mix_kernel, alpha=alpha),
        grid=(m // tm,),
        in_specs=[row(d), row(a.shape[1]), row(b.shape[1]), const(w_o.shape), const((1, d)), const((1, d))],
        out_specs=row(d),
        out_shape=jax.ShapeDtypeStruct((m, d), F32),
        compiler_params=_params("parallel"),
        name="post_mix",
    )(h, a, b, w_o, g, bb)


def _gated_post_mix(h, u, v, w_s, b_s_t, b, w_o, g, bb, *, alpha, tm):
    m, d = h.shape
    aw = u.shape[1]
    assert tm % CHUNK == 0
    row = lambda width: pl.BlockSpec((tm, width), lambda i: (i, 0))
    const = lambda shape: pl.BlockSpec(shape, lambda i: (0,) * len(shape))
    return pl.pallas_call(
        functools.partial(_gated_post_mix_kernel, alpha=alpha),
        grid=(m // tm,),
        in_specs=[row(d), row(aw), row(aw), const(w_s.shape), const(b_s_t.shape), row(b.shape[1]),
                  const(w_o.shape), const((1, d)), const((1, d))],
        out_specs=row(d),
        out_shape=jax.ShapeDtypeStruct((m, d), F32),
        scratch_shapes=[pltpu.VMEM((tm, aw), BF16)],
        compiler_params=_params("parallel"),
        name="gated_post_mix",
    )(h, u, v, w_s, b_s_t, b, w_o, g, bb)


def kernel(x_prompt, x_sample, cache_k, cache_v, page_table, ln1_g, ln1_b, ffn1_w_gate, ffn1_w_up, ffn1_w_down, w_in, gmlp_ln_g, gmlp_ln_b, gmlp_w_s, gmlp_b_s, sb_bias, w_o, ln2_g, ln2_b, ffn2_w_gate, ffn2_w_up, ffn2_w_down, ln3_g, ln3_b):
    depth = w_in.shape[0]
    alpha = (2 * depth) ** 0.25
    n, t, d = x_prompt.shape
    nb, t_new, _ = x_sample.shape
    assert t_new == 1, "the sample path handles one new position per sequence"
    page_size, n_heads, hd = cache_k.shape[2:]
    assert hd == HEAD_DIM
    aw = gmlp_ln_g.shape[1]
    row = lambda p: p.reshape(1, -1)
    tiles = _plan_tiles(d, w_in.shape[2], page_table.shape[1], page_size * n_heads * hd * 4)

    xp = x_prompt.reshape(n * t, d)
    xs = x_sample.reshape(nb * t_new, d)
    kp_l, vp_l, ks_l, vs_l, cv_l = [], [], [], [], []
    for l in range(depth):
        bf = lambda p: p[l].astype(BF16)
        wg1, wu1, wd1 = ffn1_w_gate[l], ffn1_w_up[l], ffn1_w_down[l]
        wg2, wu2, wd2 = ffn2_w_gate[l], ffn2_w_up[l], ffn2_w_down[l]
        win, wo = bf(w_in), bf(w_o)
        ffn1 = functools.partial(_ffn_ln, wg=wg1, wu=wu1, wd=wd1, g=row(ln1_g[l]), b=row(ln1_b[l]), alpha=alpha)
        ffn2 = functools.partial(_ffn_ln, wg=wg2, wu=wu2, wd=wd2, g=row(ln3_g[l]), b=row(ln3_b[l]), alpha=alpha)
        proj = functools.partial(_in_proj, w_in=win, lg=row(gmlp_ln_g[l]), lb=row(gmlp_ln_b[l]))
        mix = functools.partial(_post_mix, w_o=wo, g=row(ln2_g[l]), bb=row(ln2_b[l]), alpha=alpha)

        hp, hs = ffn1(xp, xs, tm=tiles.ffn_rows, tf=tiles.ffn_cols)

        u, v, q, k32, vb32, k16, vb16 = proj(hp, tm=tiles.proj_rows, v_dtype=BF16)
        b_out = _sb_prompt(q, k16, vb16, sb_bias[l], n=n, t=t, heads=tiles.sb_heads, tq=tiles.sb_rows)
        kp_l.append(k32.reshape(n, t, n_heads, hd))
        vp_l.append(vb32.reshape(n, t, n_heads, hd))

        u_s, v_s, q_s, k_s, vb_s, _, _ = proj(hs, tm=nb, v_dtype=F32)
        w0 = jnp.repeat(gmlp_w_s[l][:, 0, 0], A_GROUP).reshape(1, aw)
        b0 = jnp.repeat(gmlp_b_s[l][:, 0], A_GROUP).reshape(1, aw)
        a_s = _sample_gate(u_s, v_s, w0, b0)
        b_s = _sb_sample(page_table, q_s, k_s, vb_s, sb_bias[l], cache_k[l], cache_v[l], pages=tiles.sb_pages)
        h2p = _gated_post_mix(hp, u, v, gmlp_w_s[l], gmlp_b_s[l].T, b_out, wo, row(ln2_g[l]), row(ln2_b[l]),
                              alpha=alpha, tm=tiles.proj_rows)
        xp, xs = ffn2(h2p, mix(hs, a_s, b_s, tm=nb), tm=tiles.ffn_rows, tf=tiles.ffn_cols)
        ks_l.append(k_s.reshape(nb, t_new, n_heads, hd))
        vs_l.append(vb_s.reshape(nb, t_new, n_heads, hd))
        cv_l.append(v_s.reshape(nb, t_new, aw))

    return (xp.reshape(n, t, d), xs.reshape(nb, t_new, d),
            jnp.stack(kp_l), jnp.stack(vp_l), jnp.stack(ks_l), jnp.stack(vs_l), jnp.stack(cv_l))
```

```python
import functools
from typing import NamedTuple

import jax
import jax.numpy as jnp
from jax import lax
from jax.experimental import pallas as pl
from jax.experimental.pallas import tpu as pltpu

LN_EPS = 1e-5
HEAD_DIM = 128
A_GROUP = 128
CHUNK = 128

V7X_VMEM_LIMIT_BYTES = 62 * 1024 * 1024
V7X_LANES = 128
V7X_MXU_DIM = 256

F32 = jnp.float32
BF16 = jnp.bfloat16
LOG2E = 1.4426950408889634
IN_PROJ_CHUNK = V7X_MXU_DIM
MASKED_LOG2 = -1e30


class _Tiles(NamedTuple):
    ffn_rows: int
    ffn_cols: int
    proj_rows: int
    sb_rows: int
    sb_heads: int
    sb_pages: int


def _plan_tiles(d_model, in_cols, n_pages, page_bytes):
    tiles = _Tiles(ffn_rows=1024, ffn_cols=V7X_MXU_DIM, proj_rows=512, sb_rows=V7X_MXU_DIM, sb_heads=4,
                   sb_pages=n_pages)
    f32, bf16 = 4, 2
    out_bytes = 3 * f32 + 4 * bf16
    need = {
        "ffn_ln": tiles.ffn_rows * d_model * (f32 + bf16 + 2 * f32) + 2 * 3 * d_model * tiles.ffn_cols * f32,
        "in_proj": (d_model * in_cols * bf16 + tiles.proj_rows * d_model * (2 * f32 + bf16)
                    + tiles.proj_rows * (in_cols // 5) * (f32 + 2 * out_bytes)),
        "sb_sample": 2 * 2 * tiles.sb_pages * page_bytes,
    }
    for name, nbytes in need.items():
        assert nbytes <= V7X_VMEM_LIMIT_BYTES, (name, nbytes)
    return tiles


def _params(*semantics):
    return pltpu.CompilerParams(dimension_semantics=semantics,
                                vmem_limit_bytes=V7X_VMEM_LIMIT_BYTES)


def _layer_norm(y, g, b):
    mu = jnp.mean(y, axis=-1, keepdims=True)
    d = y - mu
    var = jnp.mean(d * d, axis=-1, keepdims=True)
    return d * lax.rsqrt(var + LN_EPS) * g + b


def _dot(a, b):
    return jnp.dot(a, b, preferred_element_type=F32)


def _dot_nt(a, b):
    return lax.dot_general(a, b, (((1,), (1,)), ((), ())), preferred_element_type=F32)


def _stick_logs2(z2):
    sp = jnp.log(1.0 + jnp.exp2(-jnp.abs(z2))) * LOG2E
    lb = jnp.minimum(z2, 0.0) - sp
    return lb, lb - z2


def _split_bf16(x):
    hi = x.astype(BF16)
    lo = (x - hi.astype(F32)).astype(BF16)
    return hi, lo


def _strict_after_matrix(n):
    j = lax.broadcasted_iota(jnp.int32, (n, n), 0)
    s = lax.broadcasted_iota(jnp.int32, (n, n), 1)
    return (j > s).astype(BF16)


def _ffn_ln_kernel(x_ref, xs_ref, wg_ref, wu_ref, wd_ref, g_ref, b_ref, o_ref, os_ref, xb_ref, *, alpha):
    i = pl.program_id(0)
    j = pl.program_id(1)
    tm = x_ref.shape[0]
    with_sample = i == pl.num_programs(0) - 1
    first_cols = j == 0
    last_cols = j == pl.num_programs(1) - 1

    @pl.when(first_cols)
    def _():
        xb_ref[:tm, :] = x_ref[...].astype(BF16)
        o_ref[...] = jnp.zeros_like(o_ref)

    @pl.when(first_cols & with_sample)
    def _():
        xb_ref[tm:, :] = xs_ref[...].astype(BF16)
        os_ref[...] = jnp.zeros_like(os_ref)

    def swiglu(xb):
        gate = _dot(xb, wg_ref[...].astype(BF16))
        up = _dot(xb, wu_ref[...].astype(BF16))
        h = (jax.nn.silu(gate) * up).astype(BF16)
        return _dot(h, wd_ref[...].astype(BF16))

    @pl.when(jnp.logical_not(with_sample))
    def _():
        o_ref[...] += swiglu(xb_ref[:tm, :])

    @pl.when(with_sample)
    def _():
        y = swiglu(xb_ref[...])
        o_ref[...] += y[:tm]
        os_ref[...] += y[tm:]

    def finish(x_ref, o_ref):
        y = alpha * x_ref[...] + 0.5 * o_ref[...]
        o_ref[...] = _layer_norm(y, g_ref[...], b_ref[...])

    @pl.when(last_cols)
    def _():
        finish(x_ref, o_ref)

    @pl.when(last_cols & with_sample)
    def _():
        finish(xs_ref, os_ref)


def _ffn_ln(x, xs, wg, wu, wd, g, b, *, alpha, tm, tf):
    m, d = x.shape
    ms = xs.shape[0]
    d_ff = wg.shape[1]
    once = pl.Buffered(1)
    return pl.pallas_call(
        functools.partial(_ffn_ln_kernel, alpha=alpha),
        grid=(m // tm, d_ff // tf),
        in_specs=[
            pl.BlockSpec((tm, d), lambda i, j: (i, 0)),
            pl.BlockSpec((ms, d), lambda i, j: (0, 0), pipeline_mode=once),
            pl.BlockSpec((d, tf), lambda i, j: (0, j)),
            pl.BlockSpec((d, tf), lambda i, j: (0, j)),
            pl.BlockSpec((tf, d), lambda i, j: (j, 0)),
            pl.BlockSpec((1, d), lambda i, j: (0, 0)),
            pl.BlockSpec((1, d), lambda i, j: (0, 0)),
        ],
        out_specs=[pl.BlockSpec((tm, d), lambda i, j: (i, 0)),
                   pl.BlockSpec((ms, d), lambda i, j: (0, 0))],
        out_shape=[jax.ShapeDtypeStruct((m, d), F32), jax.ShapeDtypeStruct((ms, d), F32)],
        scratch_shapes=[pltpu.VMEM((tm + ms, d), BF16)],
        compiler_params=_params("arbitrary", "arbitrary"),
        name="ffn_ln",
    )(x, xs, wg, wu, wd, g, b)


def _in_proj_kernel(h_ref, hs_ref, w_ref, lg_ref, lb_ref, *refs, q_scale):
    prompt_outs, sample_outs, (hb_ref, gelu_ref) = refs[:7], refs[7:14], refs[14:]
    width = prompt_outs[0].shape[1]
    chunks = [slice(n0, n0 + IN_PROJ_CHUNK) for n0 in range(0, width, IN_PROJ_CHUNK)]

    def rows(h_ref, outs):
        u_ref, v_ref, q_ref, k32_ref, vb32_ref, k16_ref, vb16_ref = outs
        n = h_ref.shape[0]
        hb_ref[:n, :] = h_ref[...].astype(BF16)

        def project(group, cols):
            start = group * width + cols.start
            return _dot(hb_ref[:n, :], w_ref[:, start:start + IN_PROJ_CHUNK])

        for cols in chunks:
            u_ref[:, cols] = jax.nn.gelu(project(0, cols))

        total = jnp.zeros((n, 1), F32)
        for cols in chunks:
            act = jax.nn.gelu(project(1, cols))
            gelu_ref[:n, cols] = act
            total = total + jnp.sum(act, axis=1, keepdims=True)
        d = gelu_ref[:n, :] - total * (1.0 / width)
        var = jnp.mean(d * d, axis=-1, keepdims=True)
        v_ref[...] = (d * lax.rsqrt(var + LN_EPS) * lg_ref[...] + lb_ref[...]).astype(v_ref.dtype)

        for cols in chunks:
            q_ref[:, cols] = (project(2, cols) * q_scale).astype(BF16)
        for group, (f32_ref, bf16_ref) in ((3, (k32_ref, k16_ref)), (4, (vb32_ref, vb16_ref))):
            for cols in chunks:
                z = project(group, cols)
                f32_ref[:, cols] = z
                bf16_ref[:, cols] = z.astype(BF16)

    rows(h_ref, prompt_outs)

    @pl.when(pl.program_id(0) == pl.num_programs(0) - 1)
    def _():
        rows(hs_ref, sample_outs)


def _in_proj(h, hs, w_in, lg, lb, *, tm):
    m, d = h.shape
    ms = hs.shape[0]
    w = w_in.shape[1] // 5
    row = lambda i: (i, 0)
    const = lambda i: (0, 0)
    once = pl.Buffered(1)
    dtypes = lambda v_dtype: [F32, v_dtype, BF16, F32, F32, BF16, BF16]
    return pl.pallas_call(
        functools.partial(_in_proj_kernel, q_scale=HEAD_DIM ** -0.5 * LOG2E),
        grid=(m // tm,),
        in_specs=[
            pl.BlockSpec((tm, d), row),
            pl.BlockSpec((ms, d), const, pipeline_mode=once),
            pl.BlockSpec(w_in.shape, const, pipeline_mode=once),
            pl.BlockSpec((1, w), const),
            pl.BlockSpec((1, w), const),
        ],
        out_specs=[pl.BlockSpec((tm, w), row)] * 7 + [pl.BlockSpec((ms, w), const)] * 7,
        out_shape=([jax.ShapeDtypeStruct((m, w), dt) for dt in dtypes(BF16)]
                   + [jax.ShapeDtypeStruct((ms, w), dt) for dt in dtypes(F32)]),
        scratch_shapes=[pltpu.VMEM((tm, d), BF16), pltpu.VMEM((tm, w), F32)],
        compiler_params=_params("arbitrary"),
        name="in_proj",
    )(h, hs, w_in, lg, lb)


def _gate_rows(u_ref, v_ref, ws_ref, bst_ref, a_ref):
    n_groups = ws_ref.shape[0]
    chunks = u_ref.shape[0] // CHUNK
    t = lax.broadcasted_iota(jnp.int32, (CHUNK, CHUNK), 0)
    s = lax.broadcasted_iota(jnp.int32, (CHUNK, CHUNK), 1)
    causal = s <= t
    for g in range(n_groups):
        w = jnp.where(causal, ws_ref[g], 0.0).astype(BF16)
        bias = bst_ref[:, g:g + 1]
        cols = slice(g * A_GROUP, (g + 1) * A_GROUP)
        for c in range(chunks):
            rows = slice(c * CHUNK, (c + 1) * CHUNK)
            mixed = _dot(w, v_ref[rows, cols]) + bias
            a_ref[rows, cols] = (u_ref[rows, cols] * mixed).astype(a_ref.dtype)


def _sample_gate_kernel(u_ref, v_ref, w0_ref, b0_ref, a_ref):
    a_ref[...] = (u_ref[...] * (w0_ref[...] * v_ref[...] + b0_ref[...])).astype(a_ref.dtype)


def _sample_gate(u, v, w0, b0):
    m, w = u.shape
    full = lambda shape: pl.BlockSpec(shape, lambda i: (0, 0))
    return pl.pallas_call(
        _sample_gate_kernel,
        grid=(1,),
        in_specs=[full((m, w)), full((m, w)), full((1, w)), full((1, w))],
        out_specs=full((m, w)),
        out_shape=jax.ShapeDtypeStruct((m, w), BF16),
        compiler_params=_params("arbitrary"),
        name="sample_gate",
    )(u, v, w0, b0)


def _sb_prompt_kernel(bias_ref, q_ref, k_ref, v_ref, o_ref, lb_ref, add_ref, acc_ref, *, heads, tq):
    hg = pl.program_id(1)
    qi = pl.program_id(2)
    n_blocks = qi + 1
    after = _strict_after_matrix(tq)
    row = lax.broadcasted_iota(jnp.int32, (tq, tq), 0)
    col = lax.broadcasted_iota(jnp.int32, (tq, tq), 1)
    diag_mask = col < row

    cols = [slice(hh * HEAD_DIM, (hh + 1) * HEAD_DIM) for hh in range(heads)]
    bias2 = [bias_ref[hg * heads + hh] * LOG2E for hh in range(heads)]

    def key_rows(s):
        return pl.ds(pl.multiple_of((n_blocks - 1 - s) * tq, tq), tq)

    def logits(s):
        return [_dot_nt(q_ref[:, c], k_ref[key_rows(s), c]) for c in cols]

    def stash_logs(z2, slot, mask):
        sums = []
        for hh, (z, b) in enumerate(zip(z2, bias2)):
            lb, l1m = _stick_logs2(z + b)
            if mask is not None:
                lb = jnp.where(mask, lb, MASKED_LOG2)
                l1m = jnp.where(mask, l1m, 0.0)
            lb_ref[slot, hh] = lb
            add_ref[slot, hh] = l1m.astype(BF16)
            sums.append(jnp.sum(l1m, axis=1, keepdims=True))
        return sums

    def later_mass(slot):
        return [_dot(add_ref[slot, hh], after) for hh in range(heads)]

    def weights(slot, afts, runs):
        return [jnp.exp2(lb_ref[slot, hh] + aft + run).astype(BF16) for hh, (aft, run) in enumerate(zip(afts, runs))]

    def accumulate(s, probs):
        for a, c in zip(probs, cols):
            acc_ref[:, c] += _dot(a, v_ref[key_rows(s), c])

    def finish(s, slot, runs):
        accumulate(s, weights(slot, later_mass(slot), runs))

    def advance(s, slot, carry):
        runs, sums = carry
        afts = later_mass(slot)
        z_next = logits(s + 1)
        accumulate(s, weights(slot, afts, runs))
        next_sums = stash_logs(z_next, 1 - slot, None)
        return [r + x for r, x in zip(runs, sums)], next_sums

    def pair(p, carry):
        s = 2 * p
        runs, sums = carry
        afts = later_mass(0)
        z_one = logits(s + 1)
        z_two = logits(s + 2)
        accumulate(s, weights(0, afts, runs))
        sums_one = stash_logs(z_one, 1, None)
        runs_one = [r + x for r, x in zip(runs, sums)]
        afts_one = later_mass(1)
        sums_two = stash_logs(z_two, 0, None)
        accumulate(s + 1, weights(1, afts_one, runs_one))
        return [r + x for r, x in zip(runs_one, sums_one)], sums_two

    acc_ref[...] = jnp.zeros_like(acc_ref)
    carry = ([jnp.zeros((tq, 1), F32) for _ in range(heads)], stash_logs(logits(0), 0, diag_mask))
    pairs = (n_blocks - 1) // 2
    runs, sums = lax.fori_loop(0, pairs, pair, carry)
    s_next = 2 * pairs

    @pl.when(n_blocks % 2 == 1)
    def _():
        finish(s_next, 0, runs)

    @pl.when(n_blocks % 2 == 0)
    def _():
        runs_last, _ = advance(s_next, 0, (runs, sums))
        finish(s_next + 1, 1, runs_last)

    o_ref[...] = acc_ref[...].astype(o_ref.dtype)


def _sb_prompt(q, k, v, sb_bias, *, n, t, heads, tq):
    m, w = q.shape
    gw = heads * HEAD_DIM
    nq = t // tq
    return pl.pallas_call(
        functools.partial(_sb_prompt_kernel, heads=heads, tq=tq),
        grid=(n, w // gw, nq),
        in_specs=[
            pl.BlockSpec(memory_space=pltpu.SMEM),
            pl.BlockSpec((tq, gw), lambda b, h, i: (b * nq + i, h)),
            pl.BlockSpec((t, gw), lambda b, h, i: (b, h)),
            pl.BlockSpec((t, gw), lambda b, h, i: (b, h)),
        ],
        out_specs=pl.BlockSpec((tq, gw), lambda b, h, i: (b * nq + i, h)),
        out_shape=jax.ShapeDtypeStruct((m, w), BF16),
        scratch_shapes=[
            pltpu.VMEM((2, heads, tq, tq), F32),
            pltpu.VMEM((2, heads, tq, tq), BF16),
            pltpu.VMEM((tq, gw), F32),
        ],
        compiler_params=_params("parallel", "parallel", "arbitrary"),
        name="sb_prompt",
    )(sb_bias, q, k, v)


def _sum_over_rows(x):
    n = x.shape[0]
    shift = n // 2
    while shift:
        x = x + pltpu.roll(x, shift, 0)
        shift //= 2
    return x


def _sum_of_later_rows(x, row):
    n = x.shape[0]
    up = lambda y, s: jnp.where(row < n - s, pltpu.roll(y, n - s, 0), 0.0)
    y = up(x, 1)
    shift = 1
    while shift < n:
        y = y + up(y, shift)
        shift *= 2
    return y


def _sb_sample_kernel(pt_ref, q_ref, kn_ref, vn_ref, bias_col_ref, bias_lane_ref, *refs, pages, past):
    del pt_ref
    k_refs = refs[:pages]
    v_refs = refs[pages:2 * pages]
    o_ref, run_ref, acc_ref = refs[2 * pages:]
    g = pl.program_id(1)
    n_heads, hd = acc_ref.shape
    n_row, lanes = run_ref.shape
    page_size = k_refs[0].shape[0] // n_heads
    per_row = lanes // n_heads
    q_pos = past
    q = q_ref[0]

    row = lax.broadcasted_iota(jnp.int32, (n_row, lanes), 0)
    lane = lax.broadcasted_iota(jnp.int32, (n_row, lanes), 1)
    own = (lax.broadcasted_iota(jnp.int32, (n_heads, lanes), 1) % n_heads
           == lax.broadcasted_iota(jnp.int32, (n_heads, lanes), 0))

    @pl.when(g == 0)
    def _():
        z = jnp.sum(q.astype(F32) * kn_ref[0], axis=1, keepdims=True) + bias_col_ref[...] * LOG2E
        lb, l1m = _stick_logs2(z)
        seen = jnp.full(z.shape, past, jnp.int32) < q_pos
        l1m = jnp.where(seen, l1m, 0.0)
        run_ref[...] = _sum_over_rows(jnp.where(own, l1m, 0.0))
        acc_ref[...] = jnp.where(seen, jnp.exp2(lb), 0.0) * vn_ref[0]

    src = lax.broadcasted_iota(jnp.int32, (lanes, lanes), 0)
    dst = lax.broadcasted_iota(jnp.int32, (lanes, lanes), 1)
    same_head = src % n_heads == dst % n_heads
    scan = jnp.concatenate([(same_head & (src > dst)).astype(BF16), same_head.astype(BF16)], axis=1)

    bias_row = bias_lane_ref[...] * LOG2E
    first_page = (pl.num_programs(1) - 1 - g) * pages
    run = run_ref[...]
    acc = acc_ref[...]
    scores = [_dot_nt(q, k_ref[...].astype(BF16)) for k_ref in k_refs]
    lbs, seens, sums = [], [], []
    for j in range(pages):
        z = jnp.zeros((n_row, lanes), F32)
        for i in range(n_row):
            mine = jnp.where(own, scores[j][:, i * lanes:(i + 1) * lanes], 0.0)
            z = jnp.where(row == i, _sum_over_rows(mine), z)
        lb, l1m = _stick_logs2(z + bias_row)
        k_pos = (first_page + j) * page_size + row * per_row + lane // n_heads
        seen = k_pos < q_pos
        hi, lo = _split_bf16(jnp.where(seen, l1m, 0.0))
        both = _dot(jnp.concatenate([hi, lo], axis=0), scan)
        lbs.append(lb)
        seens.append(seen)
        sums.append(both[:n_row] + both[n_row:])
    wides = [None] * pages
    for j in reversed(range(pages)):
        in_row, row_total = sums[j][:, :lanes], sums[j][:, lanes:]
        later_rows = _sum_of_later_rows(row_total, row)
        a = jnp.where(seens[j], jnp.exp2(lbs[j] + in_row + later_rows + run), 0.0)
        run = run + jnp.broadcast_to(later_rows[0:1] + row_total[0:1], run.shape)
        wides[j] = jnp.concatenate(
            [jnp.where(own, jnp.broadcast_to(a[i:i + 1], (n_heads, lanes)), 0.0) for i in range(n_row)],
            axis=1).astype(BF16)
    for wide, v_ref in zip(wides, v_refs):
        acc = acc + _dot(wide, v_ref[...].astype(BF16))
    run_ref[...] = run
    acc_ref[...] = acc

    @pl.when(g == pl.num_programs(1) - 1)
    def _():
        o_ref[0] = acc.astype(o_ref.dtype)


def _sb_sample(page_table, q, k_new, v_new, sb_bias, cache_k, cache_v, *, pages):
    nb, n_pages = page_table.shape
    n_phys, page_size, n_heads, hd = cache_k.shape
    page_rows = page_size * n_heads
    assert page_rows % V7X_LANES == 0 and V7X_LANES % n_heads == 0 and n_pages % pages == 0
    n_row = page_rows // V7X_LANES
    assert n_row == n_heads, "the per-head running mass shares the (n_row, lanes) tile shape"
    groups = n_pages // pages
    tok = lambda b, g, pt: (b, 0, 0)
    const = lambda b, g, pt: (0, 0)
    heads = lambda x: x.reshape(nb, n_heads, hd)
    ck = cache_k.reshape(n_phys * page_rows, hd)
    cv = cache_v.reshape(n_phys * page_rows, hd)

    def page_spec(j):
        return pl.BlockSpec((page_rows, hd),
                            lambda b, g, pt: (pt[b * n_pages + (groups - 1 - g) * pages + j], 0))

    grid_spec = pltpu.PrefetchScalarGridSpec(
        num_scalar_prefetch=1,
        grid=(nb, groups),
        in_specs=[
            pl.BlockSpec((1, n_heads, hd), tok),
            pl.BlockSpec((1, n_heads, hd), tok),
            pl.BlockSpec((1, n_heads, hd), tok),
            pl.BlockSpec((n_heads, 1), const),
            pl.BlockSpec((1, V7X_LANES), const),
        ] + [page_spec(j) for j in range(pages)] * 2,
        out_specs=pl.BlockSpec((1, n_heads, hd), tok),
        scratch_shapes=[pltpu.VMEM((n_row, V7X_LANES), F32), pltpu.VMEM((n_heads, hd), F32)],
    )
    out = pl.pallas_call(
        functools.partial(_sb_sample_kernel, pages=pages, past=n_pages * page_size),
        grid_spec=grid_spec,
        out_shape=jax.ShapeDtypeStruct((nb, n_heads, hd), BF16),
        compiler_params=_params("parallel", "arbitrary"),
        name="sb_sample",
    )(page_table.reshape(-1), heads(q), heads(k_new), heads(v_new),
      sb_bias.reshape(n_heads, 1), jnp.tile(sb_bias, V7X_LANES // n_heads).reshape(1, V7X_LANES),
      *([ck] * pages), *([cv] * pages))
    return out.reshape(nb, n_heads * hd)


def _post_mix_kernel(h_ref, a_ref, b_ref, wo_ref, g_ref, bb_ref, o_ref, *, alpha):
    aw = a_ref.shape[1]
    mix = _dot(b_ref[...], wo_ref[aw:, :]) + _dot(a_ref[...], wo_ref[:aw, :])
    o_ref[...] = _layer_norm(alpha * h_ref[...] + mix, g_ref[...], bb_ref[...])


def _gated_post_mix_kernel(h_ref, u_ref, v_ref, ws_ref, bst_ref, b_ref, wo_ref, g_ref, bb_ref, o_ref, a_ref, *,
                           alpha):
    aw = a_ref.shape[1]
    mix_b = _dot(b_ref[...], wo_ref[aw:, :])
    _gate_rows(u_ref, v_ref, ws_ref, bst_ref, a_ref)
    mix = mix_b + _dot(a_ref[...], wo_ref[:aw, :])
    o_ref[...] = _layer_norm(alpha * h_ref[...] + mix, g_ref[...], bb_ref[...])


def _post_mix(h, a, b, w_o, g, bb, *, alpha, tm):
    m, d = h.shape
    row = lambda width: pl.BlockSpec((tm, width), lambda i: (i, 0))
    const = lambda shape: pl.BlockSpec(shape, lambda i: (0, 0))
    return pl.pallas_call(
        functools.partial(_post_mix_kernel, alpha=alpha),
        grid=(m // tm,),
        in_specs=[row(d), row(a.shape[1]), row(b.shape[1]), const(w_o.shape), const((1, d)), const((1, d))],
        out_specs=row(d),
        out_shape=jax.ShapeDtypeStruct((m, d), F32),
        compiler_params=_params("parallel"),
        name="post_mix",
    )(h, a, b, w_o, g, bb)


def _gated_post_mix(h, u, v, w_s, b_s_t, b, w_o, g, bb, *, alpha, tm):
    m, d = h.shape
    aw = u.shape[1]
    assert tm % CHUNK == 0
    row = lambda width: pl.BlockSpec((tm, width), lambda i: (i, 0))
    const = lambda shape: pl.BlockSpec(shape, lambda i: (0,) * len(shape))
    return pl.pallas_call(
        functools.partial(_gated_post_mix_kernel, alpha=alpha),
        grid=(m // tm,),
        in_specs=[row(d), row(aw), row(aw), const(w_s.shape), const(b_s_t.shape), row(b.shape[1]),
                  const(w_o.shape), const((1, d)), const((1, d))],
        out_specs=row(d),
        out_shape=jax.ShapeDtypeStruct((m, d), F32),
        scratch_shapes=[pltpu.VMEM((tm, aw), BF16)],
        compiler_params=_params("parallel"),
        name="gated_post_mix",
    )(h, u, v, w_s, b_s_t, b, w_o, g, bb)


def kernel(x_prompt, x_sample, cache_k, cache_v, page_table, ln1_g, ln1_b, ffn1_w_gate, ffn1_w_up, ffn1_w_down, w_in, gmlp_ln_g, gmlp_ln_b, gmlp_w_s, gmlp_b_s, sb_bias, w_o, ln2_g, ln2_b, ffn2_w_gate, ffn2_w_up, ffn2_w_down, ln3_g, ln3_b):
    depth = w_in.shape[0]
    alpha = (2 * depth) ** 0.25
    n, t, d = x_prompt.shape
    nb, t_new, _ = x_sample.shape
    assert t_new == 1, "the sample path handles one new position per sequence"
    page_size, n_heads, hd = cache_k.shape[2:]
    assert hd == HEAD_DIM
    aw = gmlp_ln_g.shape[1]
    row = lambda p: p.reshape(1, -1)
    tiles = _plan_tiles(d, w_in.shape[2], page_table.shape[1], page_size * n_heads * hd * 4)

    xp = x_prompt.reshape(n * t, d)
    xs = x_sample.reshape(nb * t_new, d)
    kp_l, vp_l, ks_l, vs_l, cv_l = [], [], [], [], []
    for l in range(depth):
        bf = lambda p: p[l].astype(BF16)
        wg1, wu1, wd1 = ffn1_w_gate[l], ffn1_w_up[l], ffn1_w_down[l]
        wg2, wu2, wd2 = ffn2_w_gate[l], ffn2_w_up[l], ffn2_w_down[l]
        win, wo = bf(w_in), bf(w_o)
        ffn1 = functools.partial(_ffn_ln, wg=wg1, wu=wu1, wd=wd1, g=row(ln1_g[l]), b=row(ln1_b[l]), alpha=alpha)
        ffn2 = functools.partial(_ffn_ln, wg=wg2, wu=wu2, wd=wd2, g=row(ln3_g[l]), b=row(ln3_b[l]), alpha=alpha)
        proj = functools.partial(_in_proj, w_in=win, lg=row(gmlp_ln_g[l]), lb=row(gmlp_ln_b[l]))
        mix = functools.partial(_post_mix, w_o=wo, g=row(ln2_g[l]), bb=row(ln2_b[l]), alpha=alpha)

        hp, hs = ffn1(xp, xs, tm=tiles.ffn_rows, tf=tiles.ffn_cols)

        (u, v, q, k32, vb32, k16, vb16,
         u_s, v_s, q_s, k_s, vb_s, _, _) = proj(hp, hs, tm=tiles.proj_rows)
        b_out = _sb_prompt(q, k16, vb16, sb_bias[l], n=n, t=t, heads=tiles.sb_heads, tq=tiles.sb_rows)
        kp_l.append(k32.reshape(n, t, n_heads, hd))
        vp_l.append(vb32.reshape(n, t, n_heads, hd))

        w0 = jnp.repeat(gmlp_w_s[l][:, 0, 0], A_GROUP).reshape(1, aw)
        b0 = jnp.repeat(gmlp_b_s[l][:, 0], A_GROUP).reshape(1, aw)
        a_s = _sample_gate(u_s, v_s, w0, b0)
        b_s = _sb_sample(page_table, q_s, k_s, vb_s, sb_bias[l], cache_k[l], cache_v[l], pages=tiles.sb_pages)
        h2p = _gated_post_mix(hp, u, v, gmlp_w_s[l], gmlp_b_s[l].T, b_out, wo, row(ln2_g[l]), row(ln2_b[l]),
                              alpha=alpha, tm=tiles.proj_rows)
        xp, xs = ffn2(h2p, mix(hs, a_s, b_s, tm=nb), tm=tiles.ffn_rows, tf=tiles.ffn_cols)
        ks_l.append(k_s.reshape(nb, t_new, n_heads, hd))
        vs_l.append(vb_s.reshape(nb, t_new, n_heads, hd))
        cv_l.append(v_s.reshape(nb, t_new, aw))

    return (xp.reshape(n, t, d), xs.reshape(nb, t_new, d),
            jnp.stack(kp_l), jnp.stack(vp_l), jnp.stack(ks_l), jnp.stack(vs_l), jnp.stack(cv_l))
```
